```python
import math
import jax, jax.numpy as jnp
from jax import lax
import numpy as np

D_MODEL = 2048
BATCH = 2
SEQ = 4096
DEPTH = 4

GRID_W = 64
CTX_LEN = 256
HEAD_DIM = 128
Q_BLOCK = 128
ROPE_BASE = 10000.0
EPS = 1e-6
NEG_INF = -1e30

A_HEADS = 8
A_QK_DIM = 64
A_V_DIM = 128
B_HEADS = 8
B_KV_HEADS = 2
WINDOW = 128
NBAND = (WINDOW + Q_BLOCK - 1) // Q_BLOCK
PAD = NBAND * Q_BLOCK
BAND = (2 * NBAND + 1) * Q_BLOCK
C_HEADS = 8
C_KV_HEADS = 2
D_HEADS = 8
D_Q_RANK = 512
D_KV_RANK = 256
D_NOPE = 128
D_ROPE = 64
D_V = 128

A_Q = A_HEADS * 2 * A_QK_DIM
A_K = A_HEADS * 2 * A_QK_DIM
A_V = A_HEADS * A_V_DIM
A_IN = A_Q + A_K + A_V
B_Q = B_HEADS * HEAD_DIM
B_K = B_KV_HEADS * HEAD_DIM
B_V = B_KV_HEADS * HEAD_DIM
B_IN = B_Q + B_K + B_V
C_Q = C_HEADS * HEAD_DIM
C_K = C_KV_HEADS * HEAD_DIM
C_V = C_KV_HEADS * HEAD_DIM
C_IN = C_Q + C_K + C_V
D_IN = D_Q_RANK + D_KV_RANK + D_ROPE
MIX_WIDTH = A_HEADS * A_V_DIM + B_HEADS * HEAD_DIM
EVEN_IN = A_IN + B_IN + MIX_WIDTH
ODD_IN = C_IN + D_IN + MIX_WIDTH
N_EVEN = (DEPTH + 1) // 2
N_ODD = DEPTH // 2

kernel_name = "hybrid_diff_window_qknorm_mla_prefix_trunk"


def rmsnorm(x, g):
    xf = x.astype(jnp.float32)
    y = xf * lax.rsqrt(jnp.mean(xf * xf, axis=-1, keepdims=True) + EPS)
    return (y * g.astype(jnp.float32)).astype(x.dtype)


def axial_angles(n_tok, dim):
    rows = n_tok // GRID_W
    row = jnp.repeat(jnp.arange(rows), GRID_W).astype(jnp.float32)
    col = jnp.tile(jnp.arange(GRID_W), rows).astype(jnp.float32)
    n_freq = dim // 4
    inv = ROPE_BASE ** (-jnp.arange(n_freq, dtype=jnp.float32) / n_freq)
    ang = jnp.concatenate([row[:, None] * inv[None], col[:, None] * inv[None]], axis=-1)
    return jnp.cos(ang), jnp.sin(ang)


def apply_rope(x, cs):
    cos, sin = cs
    shape = (1, cos.shape[0]) + (1,) * (x.ndim - 3) + (cos.shape[1],)
    cos = cos.reshape(shape).astype(x.dtype)
    sin = sin.reshape(shape).astype(x.dtype)
    x1, x2 = jnp.split(x, 2, axis=-1)
    return jnp.concatenate([x1 * cos - x2 * sin, x1 * sin + x2 * cos], axis=-1)


def softmax_with_sink(s, sink):
    full = jnp.concatenate([s, jnp.broadcast_to(sink, s.shape[:-1] + (1,)).astype(jnp.float32)], axis=-1)
    return jax.nn.softmax(full, axis=-1)[..., :-1]


def attend(q, k, v, scale):
    b, lq, hq, dk = q.shape
    hk = k.shape[2]
    qg = q.reshape(b, lq, hk, hq // hk, dk)
    s = jnp.einsum('bqhgd,bkhd->bhgqk', qg, k).astype(jnp.float32) * scale
    p = jax.nn.softmax(s, axis=-1).astype(v.dtype)
    o = jnp.einsum('bhgqk,bkhd->bqhgd', p, v)
    return o.reshape(b, lq, hq, v.shape[-1])


def diff_attend(q1, q2, k1, k2, v, lam, scale):
    p1 = jax.nn.softmax(jnp.einsum('bqhd,bkhd->bhqk', q1, k1).astype(jnp.float32) * scale, axis=-1)
    p2 = jax.nn.softmax(jnp.einsum('bqhd,bkhd->bhqk', q2, k2).astype(jnp.float32) * scale, axis=-1)
    p = (p1 - lam * p2).astype(v.dtype)
    return jnp.einsum('bhqk,bkhd->bqhd', p, v)


def over_query_blocks(fn, *qs):
    b, s = qs[0].shape[:2]
    nb = s // Q_BLOCK
    blocks = tuple(jnp.moveaxis(q.reshape((b, nb, Q_BLOCK) + q.shape[2:]), 1, 0) for q in qs)
    out = lax.map(lambda args: fn(*args), blocks)
    return jnp.moveaxis(out, 0, 1).reshape((b, s) + out.shape[3:])


def band_mask(nb, s):
    qpos = jnp.arange(nb)[:, None, None] * Q_BLOCK + jnp.arange(Q_BLOCK)[None, :, None]
    kpos = jnp.arange(nb)[:, None, None] * Q_BLOCK - PAD + jnp.arange(BAND)[None, None, :]
    return (jnp.abs(kpos - qpos) <= WINDOW) & (kpos >= 0) & (kpos < s)


def mixer_a(u_lat, u_ctx, cs64, lam, lam_init, sub_g, with_ctx):
    def heads(u):
        b, l = u.shape[:2]
        q, k, v = jnp.split(u, [A_Q, A_Q + A_K], axis=-1)
        q = q.reshape(b, l, A_HEADS, 2 * A_QK_DIM)
        k = k.reshape(b, l, A_HEADS, 2 * A_QK_DIM)
        v = v.reshape(b, l, A_HEADS, A_V_DIM)
        return q[..., :A_QK_DIM], q[..., A_QK_DIM:], k[..., :A_QK_DIM], k[..., A_QK_DIM:], v

    q1, q2, k1, k2, v = heads(u_lat)
    q1, q2, k1, k2 = (apply_rope(t, cs64) for t in (q1, q2, k1, k2))
    cq1, cq2, ck1, ck2, cv = heads(u_ctx)
    k1a = jnp.concatenate([ck1, k1], axis=1)
    k2a = jnp.concatenate([ck2, k2], axis=1)
    va = jnp.concatenate([cv, v], axis=1)
    scale = A_QK_DIM ** -0.5

    def post(o):
        b, l = o.shape[:2]
        return (rmsnorm(o, sub_g) * (1.0 - lam_init)).reshape(b, l, A_HEADS * A_V_DIM)

    o = over_query_blocks(lambda a1, a2: diff_attend(a1, a2, k1a, k2a, va, lam, scale), q1, q2)
    y = post(o)
    yc = post(diff_attend(cq1, cq2, ck1, ck2, cv, lam, scale)) if with_ctx else None
    return y, yc


def mixer_b(u_lat, u_ctx, cs128, sink, with_ctx):
    g = B_HEADS // B_KV_HEADS

    def heads(u):
        b, l = u.shape[:2]
        q, k, v = jnp.split(u, [B_Q, B_Q + B_K], axis=-1)
        return (q.reshape(b, l, B_KV_HEADS, g, HEAD_DIM),
                k.reshape(b, l, B_KV_HEADS, HEAD_DIM),
                v.reshape(b, l, B_KV_HEADS, HEAD_DIM))

    q, k, v = heads(u_lat)
    q, k = apply_rope(q, cs128), apply_rope(k, cs128)
    cq, ck, cv = heads(u_ctx)
    b, s = q.shape[:2]
    lc = ck.shape[1]
    nb = s // Q_BLOCK
    scale = HEAD_DIM ** -0.5
    sink_hg = sink.reshape(B_KV_HEADS, g)

    def band(t):
        tp = jnp.pad(t, ((0, 0), (PAD, PAD), (0, 0), (0, 0)))
        tp = tp.reshape((b, nb + 2 * NBAND, Q_BLOCK) + t.shape[2:])
        return jnp.concatenate([tp[:, i:i + nb] for i in range(2 * NBAND + 1)], axis=2)

    qb = q.reshape(b, nb, Q_BLOCK, B_KV_HEADS, g, HEAD_DIM)
    kb, vb = band(k), band(v)
    s_win = jnp.einsum('bnqhgd,bnkhd->bnhgqk', qb, kb).astype(jnp.float32) * scale
    s_win = jnp.where(band_mask(nb, s)[None, :, None, None], s_win, NEG_INF)
    s_ctx = jnp.einsum('bnqhgd,bkhd->bnhgqk', qb, ck).astype(jnp.float32) * scale
    p = softmax_with_sink(jnp.concatenate([s_ctx, s_win], axis=-1),
                          sink_hg[None, None, :, :, None, None]).astype(v.dtype)
    o = (jnp.einsum('bnhgqk,bkhd->bnqhgd', p[..., :lc], cv)
         + jnp.einsum('bnhgqk,bnkhd->bnqhgd', p[..., lc:], vb))
    y = o.reshape(b, s, B_HEADS * HEAD_DIM)
    yc = None
    if with_ctx:
        sc = jnp.einsum('bqhgd,bkhd->bhgqk', cq, ck).astype(jnp.float32) * scale
        pc = softmax_with_sink(sc, sink_hg[None, :, :, None, None]).astype(cv.dtype)
        yc = jnp.einsum('bhgqk,bkhd->bqhgd', pc, cv).reshape(b, lc, B_HEADS * HEAD_DIM)
    return y, yc


def mixer_c(u_lat, u_ctx, cs128, q_g, k_g, with_ctx):
    def heads(u):
        b, l = u.shape[:2]
        q, k, v = jnp.split(u, [C_Q, C_Q + C_K], axis=-1)
        q = rmsnorm(q.reshape(b, l, C_HEADS, HEAD_DIM), q_g)
        k = rmsnorm(k.reshape(b, l, C_KV_HEADS, HEAD_DIM), k_g)
        return q, k, v.reshape(b, l, C_KV_HEADS, HEAD_DIM)

    q, k, v = heads(u_lat)
    q, k = apply_rope(q, cs128), apply_rope(k, cs128)
    cq, ck, cv = heads(u_ctx)
    ka = jnp.concatenate([ck, k], axis=1)
    va = jnp.concatenate([cv, v], axis=1)
    scale = HEAD_DIM ** -0.5
    b, s = q.shape[:2]
    y = over_query_blocks(lambda qb: attend(qb, ka, va, scale), q).reshape(b, s, C_HEADS * HEAD_DIM)
    yc = attend(cq, ck, cv, scale).reshape(b, ck.shape[1], C_HEADS * HEAD_DIM) if with_ctx else None
    return y, yc


def mixer_d(u_lat, u_ctx, cs64, q_a_g, kv_a_g, w_q_b, w_kv_b, with_ctx):
    def heads(u, rope):
        b, l = u.shape[:2]
        cq, ckv, kpe = jnp.split(u, [D_Q_RANK, D_Q_RANK + D_KV_RANK], axis=-1)
        q = (rmsnorm(cq, q_a_g) @ w_q_b).reshape(b, l, D_HEADS, D_NOPE + D_ROPE)
        kv = (rmsnorm(ckv, kv_a_g) @ w_kv_b).reshape(b, l, D_HEADS, D_NOPE + D_V)
        q_nope, q_pe = q[..., :D_NOPE], q[..., D_NOPE:]
        k_nope, v = kv[..., :D_NOPE], kv[..., D_NOPE:]
        kpe = kpe.reshape(b, l, 1, D_ROPE)
        if rope:
            q_pe, kpe = apply_rope(q_pe, cs64), apply_rope(kpe, cs64)
        q = jnp.concatenate([q_nope, q_pe], axis=-1)
        k = jnp.concatenate([k_nope, jnp.broadcast_to(kpe, (b, l, D_HEADS, D_ROPE))], axis=-1)
        return q, k, v

    q, k, v = heads(u_lat, True)
    cq, ck, cv = heads(u_ctx, False)
    ka = jnp.concatenate([ck, k], axis=1)
    va = jnp.concatenate([cv, v], axis=1)
    scale = (D_NOPE + D_ROPE) ** -0.5
    b, s = q.shape[:2]
    y = over_query_blocks(lambda qb: attend(qb, ka, va, scale), q).reshape(b, s, D_HEADS * D_V)
    yc = attend(cq, ck, cv, scale).reshape(b, ck.shape[1], D_HEADS * D_V) if with_ctx else None
    return y, yc


def setup_inputs(seed: int = 0) -> dict:
    key = jax.random.key(seed)
    ks = jax.random.split(key, 23)
    n = lambda k, shape, s: jax.random.normal(k, shape, jnp.float32) * s
    gain = lambda k, shape: 1.0 + 0.02 * jax.random.normal(k, shape, jnp.float32)
    return {
        "x": n(ks[0], (BATCH, SEQ, D_MODEL), 1.0),
        "c": n(ks[1], (BATCH, D_MODEL), 1.0),
        "ctx": n(ks[2], (BATCH, CTX_LEN, D_MODEL), 1.0),
        "c_ctx": n(ks[3], (D_MODEL,), 1.0),
        "w_mod": n(ks[4], (DEPTH, D_MODEL, 3 * D_MODEL), 0.5 * D_MODEL ** -0.5),
        "b_mod": n(ks[5], (DEPTH, 3 * D_MODEL), 0.01),
        "norm_g": gain(ks[6], (DEPTH, D_MODEL)),
        "w_o": n(ks[7], (DEPTH, MIX_WIDTH, D_MODEL), MIX_WIDTH ** -0.5),
        "final_g": gain(ks[8], (D_MODEL,)),
        "e_w_in": n(ks[9], (N_EVEN, D_MODEL, EVEN_IN), D_MODEL ** -0.5),
        "a_lam_q1": n(ks[10], (N_EVEN, A_QK_DIM), 0.1),
        "a_lam_k1": n(ks[11], (N_EVEN, A_QK_DIM), 0.1),
        "a_lam_q2": n(ks[12], (N_EVEN, A_QK_DIM), 0.1),
        "a_lam_k2": n(ks[13], (N_EVEN, A_QK_DIM), 0.1),
        "a_sub_g": gain(ks[14], (N_EVEN, A_V_DIM)),
        "b_sink": n(ks[15], (N_EVEN, B_HEADS), 1.0),
        "o_w_in": n(ks[16], (N_ODD, D_MODEL, ODD_IN), D_MODEL ** -0.5),
        "c_q_g": gain(ks[17], (N_ODD, HEAD_DIM)),
        "c_k_g": gain(ks[18], (N_ODD, HEAD_DIM)),
        "d_q_a_g": gain(ks[19], (N_ODD, D_Q_RANK)),
        "d_kv_a_g": gain(ks[20], (N_ODD, D_KV_RANK)),
        "d_w_q_b": n(ks[21], (N_ODD, D_Q_RANK, D_HEADS * (D_NOPE + D_ROPE)), D_Q_RANK ** -0.5),
        "d_w_kv_b": n(ks[22], (N_ODD, D_KV_RANK, D_HEADS * (D_NOPE + D_V)), D_KV_RANK ** -0.5),
    }


def reference(x, c, ctx, c_ctx, w_mod, b_mod, norm_g, w_o, final_g,
              e_w_in, a_lam_q1, a_lam_k1, a_lam_q2, a_lam_k2, a_sub_g, b_sink,
              o_w_in, c_q_g, c_k_g, d_q_a_g, d_kv_a_g, d_w_q_b, d_w_kv_b):
    s = x.shape[1]
    cs128 = axial_angles(s, HEAD_DIM)
    cs64 = axial_angles(s, A_QK_DIM)
    silu_c = jax.nn.silu(c)
    silu_cc = jax.nn.silu(c_ctx)
    cx = ctx
    for l in range(DEPTH):
        with_ctx = l < DEPTH - 1
        i = l // 2
        mod = silu_c @ w_mod[l] + b_mod[l]
        shift, scale, gate = jnp.split(mod[:, None, :], 3, axis=-1)
        cmod = silu_cc @ w_mod[l] + b_mod[l]
        cshift, cscale, cgate = jnp.split(cmod, 3)
        h = rmsnorm(x, norm_g[l]) * (1.0 + scale) + shift
        hc = rmsnorm(cx, norm_g[l]) * (1.0 + cscale) + cshift
        if l % 2 == 0:
            u = h @ e_w_in[i]
            uc = hc @ e_w_in[i]
            ua, ub, ug = jnp.split(u, [A_IN, A_IN + B_IN], axis=-1)
            uca, ucb, ucg = jnp.split(uc, [A_IN, A_IN + B_IN], axis=-1)
            lam_init = 0.8 - 0.6 * math.exp(-0.3 * l)
            lam = (jnp.exp(jnp.sum(a_lam_q1[i].astype(jnp.float32) * a_lam_k1[i].astype(jnp.float32)))
                   - jnp.exp(jnp.sum(a_lam_q2[i].astype(jnp.float32) * a_lam_k2[i].astype(jnp.float32)))
                   + lam_init)
            y1, yc1 = mixer_a(ua, uca, cs64, lam, lam_init, a_sub_g[i], with_ctx)
            y2, yc2 = mixer_b(ub, ucb, cs128, b_sink[i], with_ctx)
        else:
            u = h @ o_w_in[i]
            uc = hc @ o_w_in[i]
            ucc_, ud, ug = jnp.split(u, [C_IN, C_IN + D_IN], axis=-1)
            ucc2, ucd, ucg = jnp.split(uc, [C_IN, C_IN + D_IN], axis=-1)
            y1, yc1 = mixer_c(ucc_, ucc2, cs128, c_q_g[i], c_k_g[i], with_ctx)
            y2, yc2 = mixer_d(ud, ucd, cs64, d_q_a_g[i], d_kv_a_g[i], d_w_q_b[i], d_w_kv_b[i], with_ctx)
        y = jnp.concatenate([y1, y2], axis=-1) * jax.nn.silu(ug)
        x = x + gate * (y @ w_o[l])
        if with_ctx:
            yc = jnp.concatenate([yc1, yc2], axis=-1) * jax.nn.silu(ucg)
            cx = cx + cgate * (yc @ w_o[l])
    return rmsnorm(x, final_g)
```

```python
import functools
import math

import jax
import jax.numpy as jnp
from jax import lax
from jax.experimental import pallas as pl
from jax.experimental.pallas import tpu as pltpu

F32 = jnp.float32
BF16 = jnp.bfloat16

D_MODEL = 2048
BATCH = 2
SEQ = 4096
DEPTH = 4
GRID_W = 64
CTX_LEN = 256
HEAD_DIM = 128
ROPE_BASE = 10000.0
EPS = 1e-6
NEG_INF = -1e30
WINDOW = 128

A_HEADS = 8
A_QK_DIM = 64
B_HEADS = 8
B_KV_HEADS = 2
C_HEADS = 8
C_KV_HEADS = 2
D_HEADS = 8
D_Q_RANK = 512
D_KV_RANK = 256
D_NOPE = 128
D_ROPE = 64

A_IN = 3072
B_IN = 1536
C_IN = 1536
D_IN = 832
MIX_WIDTH = 2048
EVEN_IN = A_IN + B_IN + MIX_WIDTH
ODD_IN = C_IN + D_IN + MIX_WIDTH
ODD_PAD = 4608

LOG2E = math.log2(math.e)
LANES = 128
KEY_CHUNK = 256
N_LAT_ROWS = BATCH * SEQ
N_CTX_ROWS = BATCH * CTX_LEN
VMEM_LIMIT = 48 * 1024 * 1024


def _params(n_grid_dims, vmem=VMEM_LIMIT):
    return pltpu.CompilerParams(
        dimension_semantics=("arbitrary",) * n_grid_dims, vmem_limit_bytes=vmem)


def _silu(v):
    return v * (1.0 / (1.0 + jnp.exp(-v)))


def _rms(v, g):
    return v * lax.rsqrt(jnp.mean(v * v, axis=-1, keepdims=True) + EPS) * g


def _mod_kernel(cc_ref, w_ref, b_ref, o_ref):
    s = _silu(cc_ref[...]).astype(BF16)
    o_ref[0] = jnp.dot(s, w_ref[0].astype(BF16), preferred_element_type=F32) + b_ref[0]


def _modulation(cc, w_mod, b_mod):
    tn = 1024
    return pl.pallas_call(
        _mod_kernel,
        grid=(DEPTH, 3 * D_MODEL // tn),
        in_specs=[
            pl.BlockSpec((8, D_MODEL), lambda l, j: (0, 0)),
            pl.BlockSpec((1, D_MODEL, tn), lambda l, j: (l, 0, j)),
            pl.BlockSpec((1, 1, tn), lambda l, j: (l, 0, j)),
        ],
        out_specs=pl.BlockSpec((1, 8, tn), lambda l, j: (l, 0, j)),
        out_shape=jax.ShapeDtypeStruct((DEPTH, 8, 3 * D_MODEL), F32),
        compiler_params=_params(2),
        name="modulation",
    )(cc, w_mod, b_mod.reshape(DEPTH, 1, 3 * D_MODEL))


def _in_proj_kernel(x_ref, sc_ref, sh_ref, g_ref, w_ref, o_ref, h_ref):
    @pl.when(pl.program_id(1) == 0)
    def _():
        h = _rms(x_ref[...], g_ref[...]) * (1.0 + sc_ref[0]) + sh_ref[0]
        h_ref[...] = h.astype(BF16)

    o_ref[...] = jnp.dot(h_ref[...], w_ref[...], preferred_element_type=F32).astype(BF16)


def _in_proj(x, scale, shift, g, w, name):
    rows, n_out = x.shape[0], w.shape[1]
    tm, tn = 512, 512
    tiles_per_mod = rows // tm // scale.shape[0]
    return pl.pallas_call(
        _in_proj_kernel,
        grid=(rows // tm, n_out // tn),
        in_specs=[
            pl.BlockSpec((tm, D_MODEL), lambda i, j: (i, 0)),
            pl.BlockSpec((1, 1, D_MODEL), lambda i, j: (i // tiles_per_mod, 0, 0)),
            pl.BlockSpec((1, 1, D_MODEL), lambda i, j: (i // tiles_per_mod, 0, 0)),
            pl.BlockSpec((1, D_MODEL), lambda i, j: (0, 0)),
            pl.BlockSpec((D_MODEL, tn), lambda i, j: (0, j)),
        ],
        out_specs=pl.BlockSpec((tm, tn), lambda i, j: (i, j)),
        out_shape=jax.ShapeDtypeStruct((rows, n_out), BF16),
        scratch_shapes=[pltpu.VMEM((tm, D_MODEL), BF16)],
        compiler_params=_params(2),
        name=name,
    )(x, scale, shift, g, w)


def _out_proj_kernel(*refs, final):
    if final:
        x_ref, y1_ref, y2_ref, w_ref, gate_ref, fg_ref, o_ref = refs
    else:
        x_ref, y1_ref, y2_ref, w_ref, gate_ref, o_ref = refs
    half = MIX_WIDTH // 2
    d = jnp.dot(y1_ref[...], w_ref[0:half, :], preferred_element_type=F32)
    d = d + jnp.dot(y2_ref[...], w_ref[half:MIX_WIDTH, :], preferred_element_type=F32)
    xn = x_ref[...] + gate_ref[0] * d
    if final:
        xn = _rms(xn, fg_ref[...])
    o_ref[...] = xn


def _out_proj(x, y1, y2, w, gate, final_g, name):
    rows = x.shape[0]
    tm = 512
    tiles_per_mod = rows // tm // gate.shape[0]
    half = MIX_WIDTH // 2
    in_specs = [
        pl.BlockSpec((tm, D_MODEL), lambda i: (i, 0)),
        pl.BlockSpec((tm, half), lambda i: (i, 0)),
        pl.BlockSpec((tm, half), lambda i: (i, 0)),
        pl.BlockSpec((MIX_WIDTH, D_MODEL), lambda i: (0, 0)),
        pl.BlockSpec((1, 1, D_MODEL), lambda i: (i // tiles_per_mod, 0, 0)),
    ]
    args = [x, y1, y2, w, gate]
    if final_g is not None:
        in_specs.append(pl.BlockSpec((1, D_MODEL), lambda i: (0, 0)))
        args.append(final_g)
    return pl.pallas_call(
        functools.partial(_out_proj_kernel, final=final_g is not None),
        grid=(rows // tm,),
        in_specs=in_specs,
        out_specs=pl.BlockSpec((tm, D_MODEL), lambda i: (i, 0)),
        out_shape=jax.ShapeDtypeStruct((rows, D_MODEL), F32),
        compiler_params=_params(1, 56 * 1024 * 1024),
        name=name,
    )(*args)


def _dproj_kernel(cq_ref, ckv_ref, qg_ref, kvg_ref, wq_ref, wkv_ref, q_out, kv_out):
    cq = _rms(cq_ref[...].astype(F32), qg_ref[...]).astype(BF16)
    ckv = _rms(ckv_ref[...].astype(F32), kvg_ref[...]).astype(BF16)
    q_out[...] = jnp.dot(cq, wq_ref[...], preferred_element_type=F32).astype(BF16)
    kv_out[...] = jnp.dot(ckv, wkv_ref[...], preferred_element_type=F32).astype(BF16)


def _dproj(u, q_g, kv_g, wq, wkv, name):
    rows = u.shape[0]
    tm = 512
    n_q, n_kv = wq.shape[1], wkv.shape[1]
    return pl.pallas_call(
        _dproj_kernel,
        grid=(rows // tm,),
        in_specs=[
            pl.BlockSpec((tm, D_Q_RANK), lambda i: (i, 1536 // D_Q_RANK)),
            pl.BlockSpec((tm, D_KV_RANK), lambda i: (i, 4096 // D_KV_RANK)),
            pl.BlockSpec((1, D_Q_RANK), lambda i: (0, 0)),
            pl.BlockSpec((1, D_KV_RANK), lambda i: (0, 0)),
            pl.BlockSpec((D_Q_RANK, n_q), lambda i: (0, 0)),
            pl.BlockSpec((D_KV_RANK, n_kv), lambda i: (0, 0)),
        ],
        out_specs=[
            pl.BlockSpec((tm, n_q), lambda i: (i, 0)),
            pl.BlockSpec((tm, n_kv), lambda i: (i, 0)),
        ],
        out_shape=[
            jax.ShapeDtypeStruct((rows, n_q), BF16),
            jax.ShapeDtypeStruct((rows, n_kv), BF16),
        ],
        compiler_params=_params(1),
        name=name,
    )(u, u, q_g, kv_g, wq, wkv)


def _rope_half(x, c, s):
    return x * c + pltpu.roll(x, 64, 1) * s


def _rope_quarter(x, c, sa, sb):
    return x * c + pltpu.roll(x, 96, 1) * sa + pltpu.roll(x, 32, 1) * sb


def _attend(q, k_ref, v_ref, s_ref, m_ref, mb_ref, l_ref, acc_ref, n_chunk, group):
    m_rows = q.shape[0]
    n_group = (n_chunk - 1) // group
    assert 1 + n_group * group == n_chunk

    def rows_of(c):
        start = c * KEY_CHUNK
        if not isinstance(c, int):
            start = pl.multiple_of(start, KEY_CHUNK)
        return pl.ds(start, KEY_CHUNK)

    def scores(c):
        s = lax.dot_general(q, k_ref[rows_of(c), :], (((1,), (1,)), ((), ())),
                            preferred_element_type=F32)
        s_ref[c] = s
        return jnp.maximum(s[:, :LANES], s[:, LANES:])

    m_ref[...] = scores(0)

    def max_body(g, carry):
        c0 = 1 + g * group
        mm = scores(c0)
        for t in range(1, group):
            mm = jnp.maximum(mm, scores(c0 + t))
        m_ref[...] = jnp.maximum(m_ref[...], mm)
        return carry

    if n_group:
        lax.fori_loop(0, n_group, max_body, 0)

    row_max = jnp.max(m_ref[...], axis=1, keepdims=True)
    mb_ref[...] = jnp.broadcast_to(row_max, (m_rows, KEY_CHUNK))

    def weigh(c):
        p = jnp.exp2(s_ref[c] - mb_ref[...])
        pv = jnp.dot(p.astype(BF16), v_ref[rows_of(c), :], preferred_element_type=F32)
        return p[:, :LANES] + p[:, LANES:], pv

    l0, a0 = weigh(0)
    l_ref[...] = l0
    acc_ref[...] = a0

    def sum_body(g, carry):
        c0 = 1 + g * group
        ls, av = weigh(c0)
        for t in range(1, group):
            l2, a2 = weigh(c0 + t)
            ls = ls + l2
            av = av + a2
        l_ref[...] += ls
        acc_ref[...] += av
        return carry

    if n_group:
        lax.fori_loop(0, n_group, sum_body, 0)

    denom = jnp.sum(l_ref[...], axis=1, keepdims=True)
    return acc_ref[...] * (1.0 / denom)


def _attend_scratch(m_rows, n_chunk, dv):
    return [
        pltpu.VMEM((n_chunk, m_rows, KEY_CHUNK), F32),
        pltpu.VMEM((m_rows, LANES), F32),
        pltpu.VMEM((m_rows, KEY_CHUNK), F32),
        pltpu.VMEM((m_rows, LANES), F32),
        pltpu.VMEM((m_rows, dv), F32),
    ]


def _for_row_blocks(n_rows, block, fn):
    def body(i, carry):
        fn(pl.ds(pl.multiple_of(i * block, block), block), i)
        return carry
    lax.fori_loop(0, n_rows // block, body, 0)


PREP_ROWS = 512


def _mixer_a_kernel(*refs, latent, lam_init, tq, n_chunk):
    if latent:
        (q_ref, kc_ref, vc_ref, ug_ref, lam_ref, subg_ref, kl_ref, vl_ref, tc_ref, tsa_ref,
         tsb_ref, y_ref, k1_ref, k2_ref, vs_ref, *att) = refs
    else:
        (q_ref, kc_ref, vc_ref, ug_ref, lam_ref, subg_ref, y_ref, k1_ref, k2_ref, vs_ref,
         *att) = refs
    qi = pl.program_id(2)
    lane = lax.broadcasted_iota(jnp.int32, (1, LANES), 1)
    lo = (lane < A_QK_DIM).astype(F32)
    hi = 1.0 - lo

    @pl.when(qi == 0)
    def _():
        kc = kc_ref[...].astype(F32)
        k1_ref[0:CTX_LEN, :] = (kc * lo).astype(BF16)
        k2_ref[0:CTX_LEN, :] = (kc * hi).astype(BF16)
        vs_ref[0:CTX_LEN, :] = vc_ref[...]
        if latent:
            def prep(rows, i):
                k = _rope_quarter(kl_ref[rows, :].astype(F32), tc_ref[rows, :], tsa_ref[rows, :],
                                  tsb_ref[rows, :])
                dst = pl.ds(pl.multiple_of(CTX_LEN + i * PREP_ROWS, CTX_LEN), PREP_ROWS)
                k1_ref[dst, :] = (k * lo).astype(BF16)
                k2_ref[dst, :] = (k * hi).astype(BF16)
                vs_ref[dst, :] = vl_ref[rows, :]
            _for_row_blocks(SEQ, PREP_ROWS, prep)

    q = q_ref[...].astype(F32)
    if latent:
        rows = pl.ds(pl.multiple_of(qi * tq, tq), tq)
        q = _rope_quarter(q, tc_ref[rows, :], tsa_ref[rows, :], tsb_ref[rows, :])
    q = (q * (A_QK_DIM ** -0.5 * LOG2E)).astype(BF16)
    group = 4 if n_chunk > 1 else 1
    o1 = _attend(q, k1_ref, vs_ref, *att, n_chunk, group)
    o2 = _attend(q, k2_ref, vs_ref, *att, n_chunk, group)
    lv = lam_ref[...]
    lam = (jnp.exp(jnp.sum(lv[0:1] * lv[1:2], axis=1, keepdims=True))
           - jnp.exp(jnp.sum(lv[2:3] * lv[3:4], axis=1, keepdims=True)) + lam_init)
    o = o1 - lam * o2
    y = _rms(o, subg_ref[...]) * (1.0 - lam_init)
    y_ref[...] = (y * _silu(ug_ref[...].astype(F32))).astype(BF16)


def _mixer_a(u, uc, lam_vecs, sub_g, tables, lam_init, latent):
    tq = 512 if latent else CTX_LEN
    q_src = u if latent else uc
    rows = q_src.shape[0]
    tiles = rows // BATCH // tq
    n_chunk = 1 + SEQ // KEY_CHUNK if latent else 1
    n_keys = n_chunk * KEY_CHUNK
    gate0 = (A_IN + B_IN) // LANES
    col = lambda off: (lambda b, h, qi: (b, off + h))
    in_specs = [
        pl.BlockSpec((tq, LANES), lambda b, h, qi: (b * tiles + qi, h)),
        pl.BlockSpec((CTX_LEN, LANES), col(8)),
        pl.BlockSpec((CTX_LEN, LANES), col(16)),
        pl.BlockSpec((tq, LANES), lambda b, h, qi: (b * tiles + qi, gate0 + h)),
        pl.BlockSpec((4, A_QK_DIM), lambda b, h, qi: (0, 0)),
        pl.BlockSpec((1, LANES), lambda b, h, qi: (0, 0)),
    ]
    args = [q_src, uc, uc, q_src, lam_vecs, sub_g]
    if latent:
        table = pl.BlockSpec((SEQ, LANES), lambda b, h, qi: (0, 0))
        in_specs += [pl.BlockSpec((SEQ, LANES), col(8)), pl.BlockSpec((SEQ, LANES), col(16)),
                     table, table, table]
        args += [u, u, tables["ca"], tables["saa"], tables["sab"]]
    return pl.pallas_call(
        functools.partial(_mixer_a_kernel, latent=latent, lam_init=lam_init, tq=tq,
                          n_chunk=n_chunk),
        grid=(BATCH, A_HEADS, tiles),
        in_specs=in_specs,
        out_specs=pl.BlockSpec((tq, LANES), lambda b, h, qi: (b * tiles + qi, h)),
        out_shape=jax.ShapeDtypeStruct((rows, A_HEADS * HEAD_DIM), BF16),
        scratch_shapes=[pltpu.VMEM((n_keys, LANES), BF16)] * 3 + _attend_scratch(tq, n_chunk, LANES),
        compiler_params=_params(3),
        name="mixer_a" if latent else "mixer_a_ctx",
    )(*args)


def _mixer_b_kernel(*refs, latent, tq):
    if latent:
        (sink_ref, q_ref, kc_ref, vc_ref, ug_ref, kl_ref, vl_ref, tc_ref, ts_ref, y_ref,
         ks_ref) = refs
    else:
        sink_ref, q_ref, kc_ref, vc_ref, ug_ref, y_ref = refs
    kvh = pl.program_id(1)
    qi = pl.program_id(2)
    group = B_HEADS // B_KV_HEADS
    span = tq + 2 * WINDOW

    if latent:
        @pl.when(qi == 0)
        def _():
            def prep(rows, i):
                ks_ref[rows, :] = _rope_half(kl_ref[rows, :].astype(F32), tc_ref[rows, :],
                                             ts_ref[rows, :]).astype(BF16)
            _for_row_blocks(SEQ, PREP_ROWS, prep)

        rows = pl.ds(pl.multiple_of(qi * tq, tq), tq)
        start = pl.multiple_of(jnp.clip(qi * tq - WINDOW, 0, SEQ - span), WINDOW)
        k_win = ks_ref[pl.ds(start, span), :]
        v_win = vl_ref[pl.ds(start, span), :]
        q_pos = qi * tq + lax.broadcasted_iota(jnp.int32, (tq, span), 0)
        k_pos = start + lax.broadcasted_iota(jnp.int32, (tq, span), 1)
        in_window = jnp.abs(k_pos - q_pos) <= WINDOW

    kc = kc_ref[...]
    vc = vc_ref[...]
    trans_b = (((1,), (1,)), ((), ()))
    for gi in range(group):
        cols = slice(gi * HEAD_DIM, (gi + 1) * HEAD_DIM)
        q = q_ref[:, cols].astype(F32)
        if latent:
            q = _rope_half(q, tc_ref[rows, :], ts_ref[rows, :])
        q = (q * (HEAD_DIM ** -0.5 * LOG2E)).astype(BF16)
        sink = sink_ref[kvh * group + gi] * LOG2E
        s_c = lax.dot_general(q, kc, trans_b, preferred_element_type=F32)
        m = jnp.maximum(jnp.max(s_c, axis=1, keepdims=True), sink)
        if latent:
            s_w = lax.dot_general(q, k_win, trans_b, preferred_element_type=F32)
            s_w = jnp.where(in_window, s_w, NEG_INF)
            m = jnp.maximum(m, jnp.max(s_w, axis=1, keepdims=True))
        p_c = jnp.exp2(s_c - m)
        den = jnp.sum(p_c, axis=1, keepdims=True) + jnp.exp2(sink - m)
        o = jnp.dot(p_c.astype(BF16), vc, preferred_element_type=F32)
        if latent:
            p_w = jnp.exp2(s_w - m)
            den = den + jnp.sum(p_w, axis=1, keepdims=True)
            o = o + jnp.dot(p_w.astype(BF16), v_win, preferred_element_type=F32)
        o = o * (1.0 / den)
        y_ref[:, cols] = (o * _silu(ug_ref[:, cols].astype(F32))).astype(BF16)


def _mixer_b(u, uc, sink, tables, latent):
    tq = 256
    q_src = u if latent else uc
    rows = q_src.shape[0]
    tiles = rows // BATCH // tq
    width = (B_HEADS // B_KV_HEADS) * HEAD_DIM
    q0, k0, v0 = A_IN // width, (A_IN + 1024) // LANES, (A_IN + 1280) // LANES
    gate0 = (A_IN + B_IN + 1024) // width
    col = lambda off: (lambda b, h, qi: (b, off + h))
    in_specs = [
        pl.BlockSpec(memory_space=pltpu.SMEM),
        pl.BlockSpec((tq, width), lambda b, h, qi: (b * tiles + qi, q0 + h)),
        pl.BlockSpec((CTX_LEN, LANES), col(k0)),
        pl.BlockSpec((CTX_LEN, LANES), col(v0)),
        pl.BlockSpec((tq, width), lambda b, h, qi: (b * tiles + qi, gate0 + h)),
    ]
    args = [sink, q_src, uc, uc, q_src]
    scratch = []
    if latent:
        table = pl.BlockSpec((SEQ, LANES), lambda b, h, qi: (0, 0))
        in_specs += [pl.BlockSpec((SEQ, LANES), col(k0)), pl.BlockSpec((SEQ, LANES), col(v0)),
                     table, table]
        args += [u, u, tables["c128"], tables["s128"]]
        scratch = [pltpu.VMEM((SEQ, LANES), BF16)]
    return pl.pallas_call(
        functools.partial(_mixer_b_kernel, latent=latent, tq=tq),
        grid=(BATCH, B_KV_HEADS, tiles),
        in_specs=in_specs,
        out_specs=pl.BlockSpec((tq, width), lambda b, h, qi: (b * tiles + qi, h)),
        out_shape=jax.ShapeDtypeStruct((rows, B_HEADS * HEAD_DIM), BF16),
        scratch_shapes=scratch,
        compiler_params=_params(3),
        name="mixer_b" if latent else "mixer_b_ctx",
    )(*args)


def _mixer_c_kernel(*refs, latent, tq, n_chunk):
    if latent:
        (q_ref, kc_ref, vc_ref, ug_ref, qg_ref, kg_ref, kl_ref, vl_ref, tc_ref, ts_ref, y_ref,
         ks_ref, vs_ref, *att) = refs
    else:
        q_ref, kc_ref, vc_ref, ug_ref, qg_ref, kg_ref, y_ref, ks_ref, vs_ref, *att = refs
    qi = pl.program_id(2)
    group = C_HEADS // C_KV_HEADS

    @pl.when(qi == 0)
    def _():
        ks_ref[0:CTX_LEN, :] = _rms(kc_ref[...].astype(F32), kg_ref[...]).astype(BF16)
        vs_ref[0:CTX_LEN, :] = vc_ref[...]
        if latent:
            def prep(rows, i):
                k = _rms(kl_ref[rows, :].astype(F32), kg_ref[...])
                k = _rope_half(k, tc_ref[rows, :], ts_ref[rows, :])
                dst = pl.ds(pl.multiple_of(CTX_LEN + i * PREP_ROWS, CTX_LEN), PREP_ROWS)
                ks_ref[dst, :] = k.astype(BF16)
                vs_ref[dst, :] = vl_ref[rows, :]
            _for_row_blocks(SEQ, PREP_ROWS, prep)

    if latent:
        rows = pl.ds(pl.multiple_of(qi * tq, tq), tq)
    qs = []
    for gi in range(group):
        q = _rms(q_ref[:, gi * HEAD_DIM:(gi + 1) * HEAD_DIM].astype(F32), qg_ref[...])
        if latent:
            q = _rope_half(q, tc_ref[rows, :], ts_ref[rows, :])
        qs.append((q * (HEAD_DIM ** -0.5 * LOG2E)).astype(BF16))
    o = _attend(jnp.concatenate(qs, axis=0), ks_ref, vs_ref, *att, n_chunk,
                4 if n_chunk > 1 else 1)
    for gi in range(group):
        cols = slice(gi * HEAD_DIM, (gi + 1) * HEAD_DIM)
        y_ref[:, cols] = (o[gi * tq:(gi + 1) * tq] * _silu(ug_ref[:, cols].astype(F32))).astype(BF16)


def _mixer_c(u, uc, q_g, k_g, tables, latent):
    tq = 128 if latent else CTX_LEN
    q_src = u if latent else uc
    rows = q_src.shape[0]
    tiles = rows // BATCH // tq
    group = C_HEADS // C_KV_HEADS
    width = group * HEAD_DIM
    n_chunk = 1 + SEQ // KEY_CHUNK if latent else 1
    n_keys = n_chunk * KEY_CHUNK
    k0, v0, gate0 = 1024 // LANES, 1280 // LANES, 2048 // width
    col = lambda off: (lambda b, h, qi: (b, off + h))
    in_specs = [
        pl.BlockSpec((tq, width), lambda b, h, qi: (b * tiles + qi, h)),
        pl.BlockSpec((CTX_LEN, LANES), col(k0)),
        pl.BlockSpec((CTX_LEN, LANES), col(v0)),
        pl.BlockSpec((tq, width), lambda b, h, qi: (b * tiles + qi, gate0 + h)),
        pl.BlockSpec((1, LANES), lambda b, h, qi: (0, 0)),
        pl.BlockSpec((1, LANES), lambda b, h, qi: (0, 0)),
    ]
    args = [q_src, uc, uc, q_src, q_g, k_g]
    if latent:
        table = pl.BlockSpec((SEQ, LANES), lambda b, h, qi: (0, 0))
        in_specs += [pl.BlockSpec((SEQ, LANES), col(k0)), pl.BlockSpec((SEQ, LANES), col(v0)),
                     table, table]
        args += [u, u, tables["c128"], tables["s128"]]
    return pl.pallas_call(
        functools.partial(_mixer_c_kernel, latent=latent, tq=tq, n_chunk=n_chunk),
        grid=(BATCH, C_KV_HEADS, tiles),
        in_specs=in_specs,
        out_specs=pl.BlockSpec((tq, width), lambda b, h, qi: (b * tiles + qi, h)),
        out_shape=jax.ShapeDtypeStruct((rows, C_HEADS * HEAD_DIM), BF16),
        scratch_shapes=[pltpu.VMEM((n_keys, LANES), BF16)] * 2
        + _attend_scratch(group * tq, n_chunk, LANES),
        compiler_params=_params(3),
        name="mixer_c" if latent else "mixer_c_ctx",
    )(*args)


def _mixer_d_kernel(*refs, latent, tq, n_chunk):
    if latent:
        (q_ref, knc_ref, vc_ref, pec_ref, ug_ref, knl_ref, vl_ref, pel_ref, tc_ref, tsa_ref,
         tsb_ref, y_ref, ks_ref, vs_ref, *att) = refs
    else:
        q_ref, knc_ref, vc_ref, pec_ref, ug_ref, y_ref, ks_ref, vs_ref, *att = refs
    qi = pl.program_id(2)

    @pl.when(qi == 0)
    def _():
        ks_ref[0:CTX_LEN, 0:D_NOPE] = knc_ref[...]
        ks_ref[0:CTX_LEN, D_NOPE:2 * D_NOPE] = pec_ref[...]
        vs_ref[0:CTX_LEN, :] = vc_ref[...]
        if latent:
            def prep(rows, i):
                pe = _rope_quarter(pel_ref[rows, :].astype(F32), tc_ref[rows, :], tsa_ref[rows, :],
                                   tsb_ref[rows, :])
                dst = pl.ds(pl.multiple_of(CTX_LEN + i * PREP_ROWS, CTX_LEN), PREP_ROWS)
                ks_ref[dst, 0:D_NOPE] = knl_ref[rows, :]
                ks_ref[dst, D_NOPE:2 * D_NOPE] = pe.astype(BF16)
                vs_ref[dst, :] = vl_ref[rows, :]
            _for_row_blocks(SEQ, PREP_ROWS, prep)

    scale = (D_NOPE + D_ROPE) ** -0.5 * LOG2E
    q_nope = q_ref[:, 0:D_NOPE].astype(F32)
    q_pe = q_ref[:, D_NOPE:2 * D_NOPE].astype(F32)
    if latent:
        rows = pl.ds(pl.multiple_of(qi * tq, tq), tq)
        q_pe = _rope_quarter(q_pe, tc_ref[rows, :], tsa_ref[rows, :], tsb_ref[rows, :])
    q = jnp.concatenate([(q_nope * scale).astype(BF16), (q_pe * scale).astype(BF16)], axis=1)
    o = _attend(q, ks_ref, vs_ref, *att, n_chunk, 4 if n_chunk > 1 else 1)
    y_ref[...] = (o * _silu(ug_ref[...].astype(F32))).astype(BF16)


def _mixer_d(u, uc, qd, kvd, qdc, kvdc, tables, latent):
    tq = 512 if latent else CTX_LEN
    q_src, ug_src = (qd, u) if latent else (qdc, uc)
    rows = q_src.shape[0]
    tiles = rows // BATCH // tq
    n_chunk = 1 + SEQ // KEY_CHUNK if latent else 1
    n_keys = n_chunk * KEY_CHUNK
    pe0, gate0 = 4352 // LANES, (2048 + 1024) // LANES
    in_specs = [
        pl.BlockSpec((tq, 2 * LANES), lambda b, h, qi: (b * tiles + qi, h)),
        pl.BlockSpec((CTX_LEN, LANES), lambda b, h, qi: (b, 2 * h)),
        pl.BlockSpec((CTX_LEN, LANES), lambda b, h, qi: (b, 2 * h + 1)),
        pl.BlockSpec((CTX_LEN, LANES), lambda b, h, qi: (b, pe0)),
        pl.BlockSpec((tq, LANES), lambda b, h, qi: (b * tiles + qi, gate0 + h)),
    ]
    args = [q_src, kvdc, kvdc, uc, ug_src]
    if latent:
        table = pl.BlockSpec((SEQ, LANES), lambda b, h, qi: (0, 0))
        in_specs += [
            pl.BlockSpec((SEQ, LANES), lambda b, h, qi: (b, 2 * h)),
            pl.BlockSpec((SEQ, LANES), lambda b, h, qi: (b, 2 * h + 1)),
            pl.BlockSpec((SEQ, LANES), lambda b, h, qi: (b, pe0)),
            table, table, table]
        args += [kvd, kvd, u, tables["cd"], tables["sda"], tables["sdb"]]
    return pl.pallas_call(
        functools.partial(_mixer_d_kernel, latent=latent, tq=tq, n_chunk=n_chunk),
        grid=(BATCH, D_HEADS, tiles),
        in_specs=in_specs,
        out_specs=pl.BlockSpec((tq, LANES), lambda b, h, qi: (b * tiles + qi, h)),
        out_shape=jax.ShapeDtypeStruct((rows, D_HEADS * HEAD_DIM), BF16),
        scratch_shapes=[pltpu.VMEM((n_keys, 2 * LANES), BF16), pltpu.VMEM((n_keys, LANES), BF16)]
        + _attend_scratch(tq, n_chunk, LANES),
        compiler_params=_params(3),
        name="mixer_d" if latent else "mixer_d_ctx",
    )(*args)


def _rope_tables():
    n = jnp.arange(SEQ)
    row = (n // GRID_W).astype(F32)
    col = (n % GRID_W).astype(F32)

    def cos_sin(dim):
        n_freq = dim // 4
        inv = ROPE_BASE ** (-jnp.arange(n_freq, dtype=F32) / n_freq)
        ang = jnp.concatenate([row[:, None] * inv[None], col[:, None] * inv[None]], axis=-1)
        return jnp.cos(ang), jnp.sin(ang)

    c, s = cos_sin(HEAD_DIM)
    tables = {"c128": jnp.concatenate([c, c], 1), "s128": jnp.concatenate([-s, s], 1)}
    c, s = cos_sin(A_QK_DIM)
    z = jnp.zeros_like(c)
    tables["ca"] = jnp.concatenate([c, c, c, c], 1)
    tables["saa"] = jnp.concatenate([-s, z, -s, z], 1)
    tables["sab"] = jnp.concatenate([z, s, z, s], 1)
    tables["cd"] = jnp.concatenate([c, c, z, z], 1)
    tables["sda"] = jnp.concatenate([-s, z, z, z], 1)
    tables["sdb"] = jnp.concatenate([z, s, z, z], 1)
    return tables


def _odd_weight(w):
    d0 = C_IN
    gate0 = C_IN + D_IN
    pieces = [w[:, :d0], w[:, d0:d0 + D_Q_RANK], w[:, gate0:],
              w[:, d0 + D_Q_RANK:d0 + D_Q_RANK + D_KV_RANK], w[:, d0 + D_Q_RANK + D_KV_RANK:gate0]]
    used = sum(p.shape[1] for p in pieces)
    pieces.append(jnp.zeros((D_MODEL, ODD_PAD - used), w.dtype))
    return jnp.concatenate(pieces, axis=1).astype(BF16)


def kernel(x, c, ctx, c_ctx, w_mod, b_mod, norm_g, w_o, final_g, e_w_in, a_lam_q1, a_lam_k1, a_lam_q2, a_lam_k2, a_sub_g, b_sink, o_w_in, c_q_g, c_k_g, d_q_a_g, d_kv_a_g, d_w_q_b, d_w_kv_b):
    xl = x.reshape(N_LAT_ROWS, D_MODEL)
    xc = ctx.reshape(N_CTX_ROWS, D_MODEL)
    cc = jnp.concatenate([c, c_ctx[None], jnp.zeros((8 - BATCH - 1, D_MODEL), F32)], axis=0)
    mod = _modulation(cc, w_mod, b_mod)
    tables = _rope_tables()

    for l in range(DEPTH):
        with_ctx = l < DEPTH - 1
        i = l // 2
        shift, scale, gate = (mod[l, :BATCH, k * D_MODEL:(k + 1) * D_MODEL][:, None, :]
                              for k in range(3))
        cshift, cscale, cgate = (mod[l, BATCH:BATCH + 1, k * D_MODEL:(k + 1) * D_MODEL][:, None, :]
                                 for k in range(3))
        g = norm_g[l][None]
        wo = w_o[l].astype(BF16)
        if l % 2 == 0:
            w_in = e_w_in[i].astype(BF16)
            u = _in_proj(xl, scale, shift, g, w_in, "in_proj_even")
            uc = _in_proj(xc, cscale, cshift, g, w_in, "in_proj_even_ctx")
            lam_init = 0.8 - 0.6 * math.exp(-0.3 * l)
            lam_vecs = jnp.stack([a_lam_q1[i], a_lam_k1[i], a_lam_q2[i], a_lam_k2[i]]).astype(F32)
            sub_g = a_sub_g[i][None]
            y1 = _mixer_a(u, uc, lam_vecs, sub_g, tables, lam_init, True)
            y2 = _mixer_b(u, uc, b_sink[i], tables, True)
            if with_ctx:
                yc1 = _mixer_a(u, uc, lam_vecs, sub_g, tables, lam_init, False)
                yc2 = _mixer_b(u, uc, b_sink[i], tables, False)
        else:
            w_in = _odd_weight(o_w_in[i])
            u = _in_proj(xl, scale, shift, g, w_in, "in_proj_odd")
            uc = _in_proj(xc, cscale, cshift, g, w_in, "in_proj_odd_ctx")
            wq = jnp.pad(d_w_q_b[i].reshape(D_Q_RANK, D_HEADS, D_NOPE + D_ROPE),
                         ((0, 0), (0, 0), (0, 2 * LANES - D_NOPE - D_ROPE)))
            wq = wq.reshape(D_Q_RANK, D_HEADS * 2 * LANES).astype(BF16)
            wkv = d_w_kv_b[i].astype(BF16)
            qd, kvd = _dproj(u, d_q_a_g[i][None], d_kv_a_g[i][None], wq, wkv, "dproj")
            qdc, kvdc = _dproj(uc, d_q_a_g[i][None], d_kv_a_g[i][None], wq, wkv, "dproj_ctx")
            y1 = _mixer_c(u, uc, c_q_g[i][None], c_k_g[i][None], tables, True)
            y2 = _mixer_d(u, uc, qd, kvd, qdc, kvdc, tables, True)
            if with_ctx:
                yc1 = _mixer_c(u, uc, c_q_g[i][None], c_k_g[i][None], tables, False)
                yc2 = _mixer_d(u, uc, qd, kvd, qdc, kvdc, tables, False)
        last = l == DEPTH - 1
        xl = _out_proj(xl, y1, y2, wo, gate, final_g[None] if last else None,
                       "out_proj_final" if last else "out_proj")
        if with_ctx:
            xc = _out_proj(xc, yc1, yc2, wo, cgate, None, "out_proj_ctx")
    return xl.reshape(BATCH, SEQ, D_MODEL)
```

```python
import functools
import math

import jax
import jax.numpy as jnp
from jax import lax
from jax.experimental import pallas as pl
from jax.experimental.pallas import tpu as pltpu

F32 = jnp.float32
BF16 = jnp.bfloat16

D_MODEL = 2048
BATCH = 2
SEQ = 4096
DEPTH = 4
GRID_W = 64
CTX_LEN = 256
HEAD_DIM = 128
ROPE_BASE = 10000.0
EPS = 1e-6
NEG_INF = -1e30
WINDOW = 128

A_HEADS = 8
A_QK_DIM = 64
B_HEADS = 8
B_KV_HEADS = 2
C_HEADS = 8
C_KV_HEADS = 2
D_HEADS = 8
D_Q_RANK = 512
D_KV_RANK = 256
D_NOPE = 128
D_ROPE = 64

A_IN = 3072
B_IN = 1536
C_IN = 1536
D_IN = 832
MIX_WIDTH = 2048
EVEN_IN = A_IN + B_IN + MIX_WIDTH
ODD_IN = C_IN + D_IN + MIX_WIDTH
ODD_PAD = 4608

LOG2E = math.log2(math.e)
LANES = 128
KEY_CHUNK = 256
KEY_GROUP = 1024
N_KEY_GROUP = SEQ // KEY_GROUP
N_LAT_ROWS = BATCH * SEQ
N_CTX_ROWS = BATCH * CTX_LEN
VMEM_LIMIT = 48 * 1024 * 1024


def _params(n_grid_dims, vmem=VMEM_LIMIT):
    return pltpu.CompilerParams(
        dimension_semantics=("arbitrary",) * n_grid_dims, vmem_limit_bytes=vmem)


def _silu(v):
    return v * (1.0 / (1.0 + jnp.exp(-v)))


def _rms(v, g):
    return v * lax.rsqrt(jnp.mean(v * v, axis=-1, keepdims=True) + EPS) * g


def _mod_kernel(cc_ref, w_ref, b_ref, o_ref):
    s = _silu(cc_ref[...]).astype(BF16)
    o_ref[0] = jnp.dot(s, w_ref[0].astype(BF16), preferred_element_type=F32) + b_ref[0]


def _modulation(cc, w_mod, b_mod):
    tn = 1024
    return pl.pallas_call(
        _mod_kernel,
        grid=(DEPTH, 3 * D_MODEL // tn),
        in_specs=[
            pl.BlockSpec((8, D_MODEL), lambda l, j: (0, 0)),
            pl.BlockSpec((1, D_MODEL, tn), lambda l, j: (l, 0, j)),
            pl.BlockSpec((1, 1, tn), lambda l, j: (l, 0, j)),
        ],
        out_specs=pl.BlockSpec((1, 8, tn), lambda l, j: (l, 0, j)),
        out_shape=jax.ShapeDtypeStruct((DEPTH, 8, 3 * D_MODEL), F32),
        compiler_params=_params(2),
        name="modulation",
    )(cc, w_mod, b_mod.reshape(DEPTH, 1, 3 * D_MODEL))


def _in_proj_kernel(x_ref, sc_ref, sh_ref, g_ref, w_ref, o_ref, h_ref):
    @pl.when(pl.program_id(1) == 0)
    def _():
        h = _rms(x_ref[...], g_ref[...]) * (1.0 + sc_ref[0]) + sh_ref[0]
        h_ref[...] = h.astype(BF16)

    o_ref[...] = jnp.dot(h_ref[...], w_ref[...], preferred_element_type=F32).astype(BF16)


def _in_proj(x, scale, shift, g, w, name):
    rows, n_out = x.shape[0], w.shape[1]
    tm = 512
    tn = n_out // (4 if n_out == EVEN_IN else 3)
    tiles_per_mod = rows // tm // scale.shape[0]
    return pl.pallas_call(
        _in_proj_kernel,
        grid=(rows // tm, n_out // tn),
        in_specs=[
            pl.BlockSpec((tm, D_MODEL), lambda i, j: (i, 0)),
            pl.BlockSpec((1, 1, D_MODEL), lambda i, j: (i // tiles_per_mod, 0, 0)),
            pl.BlockSpec((1, 1, D_MODEL), lambda i, j: (i // tiles_per_mod, 0, 0)),
            pl.BlockSpec((1, D_MODEL), lambda i, j: (0, 0)),
            pl.BlockSpec((D_MODEL, tn), lambda i, j: (0, j)),
        ],
        out_specs=pl.BlockSpec((tm, tn), lambda i, j: (i, j)),
        out_shape=jax.ShapeDtypeStruct((rows, n_out), BF16),
        scratch_shapes=[pltpu.VMEM((tm, D_MODEL), BF16)],
        compiler_params=_params(2),
        name=name,
    )(x, scale, shift, g, w)


def _out_proj_kernel(*refs, final):
    if final:
        x_ref, y1_ref, y2_ref, w_ref, gate_ref, fg_ref, o_ref = refs
    else:
        x_ref, y1_ref, y2_ref, w_ref, gate_ref, o_ref = refs
    half = MIX_WIDTH // 2
    d = jnp.dot(y1_ref[...], w_ref[0:half, :], preferred_element_type=F32)
    d = d + jnp.dot(y2_ref[...], w_ref[half:MIX_WIDTH, :], preferred_element_type=F32)
    xn = x_ref[...] + gate_ref[0] * d
    if final:
        xn = _rms(xn, fg_ref[...])
    o_ref[...] = xn


def _out_proj(x, y1, y2, w, gate, final_g, name):
    rows = x.shape[0]
    tm = 512
    tiles_per_mod = rows // tm // gate.shape[0]
    half = MIX_WIDTH // 2
    in_specs = [
        pl.BlockSpec((tm, D_MODEL), lambda i: (i, 0)),
        pl.BlockSpec((tm, half), lambda i: (i, 0)),
        pl.BlockSpec((tm, half), lambda i: (i, 0)),
        pl.BlockSpec((MIX_WIDTH, D_MODEL), lambda i: (0, 0)),
        pl.BlockSpec((1, 1, D_MODEL), lambda i: (i // tiles_per_mod, 0, 0)),
    ]
    args = [x, y1, y2, w, gate]
    if final_g is not None:
        in_specs.append(pl.BlockSpec((1, D_MODEL), lambda i: (0, 0)))
        args.append(final_g)
    return pl.pallas_call(
        functools.partial(_out_proj_kernel, final=final_g is not None),
        grid=(rows // tm,),
        in_specs=in_specs,
        out_specs=pl.BlockSpec((tm, D_MODEL), lambda i: (i, 0)),
        out_shape=jax.ShapeDtypeStruct((rows, D_MODEL), F32),
        compiler_params=_params(1, 56 * 1024 * 1024),
        name=name,
    )(*args)


def _dproj_kernel(cq_ref, ckv_ref, qg_ref, kvg_ref, wq_ref, wkv_ref, q_out, kv_out):
    cq = _rms(cq_ref[...].astype(F32), qg_ref[...]).astype(BF16)
    ckv = _rms(ckv_ref[...].astype(F32), kvg_ref[...]).astype(BF16)
    q_out[...] = jnp.dot(cq, wq_ref[...], preferred_element_type=F32).astype(BF16)
    kv_out[...] = jnp.dot(ckv, wkv_ref[...], preferred_element_type=F32).astype(BF16)


def _dproj(u, q_g, kv_g, wq, wkv, name):
    rows = u.shape[0]
    tm = 512
    n_q, n_kv = wq.shape[1], wkv.shape[1]
    return pl.pallas_call(
        _dproj_kernel,
        grid=(rows // tm,),
        in_specs=[
            pl.BlockSpec((tm, D_Q_RANK), lambda i: (i, 1536 // D_Q_RANK)),
            pl.BlockSpec((tm, D_KV_RANK), lambda i: (i, 4096 // D_KV_RANK)),
            pl.BlockSpec((1, D_Q_RANK), lambda i: (0, 0)),
            pl.BlockSpec((1, D_KV_RANK), lambda i: (0, 0)),
            pl.BlockSpec((D_Q_RANK, n_q), lambda i: (0, 0)),
            pl.BlockSpec((D_KV_RANK, n_kv), lambda i: (0, 0)),
        ],
        out_specs=[
            pl.BlockSpec((tm, n_q), lambda i: (i, 0)),
            pl.BlockSpec((tm, n_kv), lambda i: (i, 0)),
        ],
        out_shape=[
            jax.ShapeDtypeStruct((rows, n_q), BF16),
            jax.ShapeDtypeStruct((rows, n_kv), BF16),
        ],
        compiler_params=_params(1),
        name=name,
    )(u, u, q_g, kv_g, wq, wkv)


def _rope_half(x, c, s):
    return x * c + pltpu.roll(x, 64, 1) * s


def _rope_quarter(x, c, sa, sb):
    return x * c + pltpu.roll(x, 96, 1) * sa + pltpu.roll(x, 32, 1) * sb


def _attend(q, k_ref, v_ref, att, n_group):
    sc_ref, mb_ref = att[0], att[1]
    sl_ref = att[2] if n_group else None
    half = q.shape[0] // 2
    parts = [slice(0, half), slice(half, 2 * half)]
    trans_b = (((1,), (1,)), ((), ()))

    def lane_fold_max(s):
        parts = [s[:, j * LANES:(j + 1) * LANES] for j in range(s.shape[1] // LANES)]
        while len(parts) > 1:
            odd = [parts[-1]] if len(parts) % 2 else []
            parts = [jnp.maximum(a, b) for a, b in zip(parts[0::2], parts[1::2])] + odd
        return parts[0]

    def group_rows(g):
        return slice(KEY_CHUNK + g * KEY_GROUP, KEY_CHUNK + (g + 1) * KEY_GROUP)

    ms = []
    for r in parts:
        s = lax.dot_general(q[r], k_ref[0:KEY_CHUNK, :], trans_b, preferred_element_type=F32)
        sc_ref[r, :] = s
        ms.append(lane_fold_max(s))
    for g in range(n_group):
        for i, r in enumerate(parts):
            s = lax.dot_general(q[r], k_ref[group_rows(g), :], trans_b,
                                preferred_element_type=F32)
            sl_ref[g, r, :] = s
            ms[i] = jnp.maximum(ms[i], lane_fold_max(s))
    for m, r in zip(ms, parts):
        mb_ref[r, :] = jnp.broadcast_to(jnp.max(m, axis=1, keepdims=True), (half, LANES))

    def weights(s, mb):
        cols = [jnp.exp2(s[:, j * LANES:(j + 1) * LANES] - mb) for j in range(s.shape[1] // LANES)]
        return jnp.concatenate(cols, axis=1).astype(BF16)

    accs = [jnp.dot(weights(sc_ref[r, :], mb_ref[r, :]), v_ref[0:KEY_CHUNK, :],
                    preferred_element_type=F32) for r in parts]
    for g in range(n_group):
        for i, r in enumerate(parts):
            accs[i] = accs[i] + jnp.dot(weights(sl_ref[g, r, :], mb_ref[r, :]),
                                        v_ref[group_rows(g), :], preferred_element_type=F32)
    return jnp.concatenate([a[:, :LANES] * (1.0 / a[:, LANES:]) for a in accs], axis=0)


def _attend_scratch(m_rows, n_group):
    scratch = [pltpu.VMEM((m_rows, KEY_CHUNK), F32), pltpu.VMEM((m_rows, LANES), F32)]
    if n_group:
        scratch.append(pltpu.VMEM((n_group, m_rows, KEY_GROUP), F32))
    return scratch


def _store_values(vs_ref, rows, v):
    vs_ref[rows, 0:LANES] = v
    vs_ref[rows, LANES:2 * LANES] = jnp.ones(v.shape, BF16)


def _for_row_blocks(n_rows, block, fn):
    def body(i, carry):
        fn(pl.ds(pl.multiple_of(i * block, block), block), i)
        return carry
    lax.fori_loop(0, n_rows // block, body, 0)


PREP_ROWS = 512


def _mixer_a_kernel(*refs, latent, lam_init, tq, n_group):
    if latent:
        (q_ref, kc_ref, vc_ref, ug_ref, lam_ref, subg_ref, kl_ref, vl_ref, tc_ref, tsa_ref,
         tsb_ref, y_ref, k1_ref, k2_ref, vs_ref, *att) = refs
    else:
        (q_ref, kc_ref, vc_ref, ug_ref, lam_ref, subg_ref, y_ref, k1_ref, k2_ref, vs_ref,
         *att) = refs
    qi = pl.program_id(2)
    lane = lax.broadcasted_iota(jnp.int32, (1, LANES), 1)
    lo = (lane < A_QK_DIM).astype(F32)
    hi = 1.0 - lo

    @pl.when(qi == 0)
    def _():
        kc = kc_ref[...].astype(F32)
        k1_ref[0:CTX_LEN, :] = (kc * lo).astype(BF16)
        k2_ref[0:CTX_LEN, :] = (kc * hi).astype(BF16)
        _store_values(vs_ref, slice(0, CTX_LEN), vc_ref[...])
        if latent:
            def prep(rows, i):
                k = _rope_quarter(kl_ref[rows, :].astype(F32), tc_ref[rows, :], tsa_ref[rows, :],
                                  tsb_ref[rows, :])
                dst = pl.ds(pl.multiple_of(CTX_LEN + i * PREP_ROWS, CTX_LEN), PREP_ROWS)
                k1_ref[dst, :] = (k * lo).astype(BF16)
                k2_ref[dst, :] = (k * hi).astype(BF16)
                _store_values(vs_ref, dst, vl_ref[rows, :])
            _for_row_blocks(SEQ, PREP_ROWS, prep)

    q = q_ref[...].astype(F32)
    if latent:
        rows = pl.ds(pl.multiple_of(qi * tq, tq), tq)
        q = _rope_quarter(q, tc_ref[rows, :], tsa_ref[rows, :], tsb_ref[rows, :])
    q = (q * (A_QK_DIM ** -0.5 * LOG2E)).astype(BF16)
    o1 = _attend(q, k1_ref, vs_ref, att, n_group)
    o2 = _attend(q, k2_ref, vs_ref, att, n_group)
    lv = lam_ref[...]
    lam = (jnp.exp(jnp.sum(lv[0:1] * lv[1:2], axis=1, keepdims=True))
           - jnp.exp(jnp.sum(lv[2:3] * lv[3:4], axis=1, keepdims=True)) + lam_init)
    o = o1 - lam * o2
    y = _rms(o, subg_ref[...]) * (1.0 - lam_init)
    y_ref[...] = (y * _silu(ug_ref[...].astype(F32))).astype(BF16)


def _mixer_a(u, uc, lam_vecs, sub_g, tables, lam_init, latent):
    tq = 512 if latent else CTX_LEN
    q_src = u if latent else uc
    rows = q_src.shape[0]
    tiles = rows // BATCH // tq
    n_group = N_KEY_GROUP if latent else 0
    n_keys = KEY_CHUNK + n_group * KEY_GROUP
    gate0 = (A_IN + B_IN) // LANES
    col = lambda off: (lambda b, h, qi: (b, off + h))
    in_specs = [
        pl.BlockSpec((tq, LANES), lambda b, h, qi: (b * tiles + qi, h)),
        pl.BlockSpec((CTX_LEN, LANES), col(8)),
        pl.BlockSpec((CTX_LEN, LANES), col(16)),
        pl.BlockSpec((tq, LANES), lambda b, h, qi: (b * tiles + qi, gate0 + h)),
        pl.BlockSpec((4, A_QK_DIM), lambda b, h, qi: (0, 0)),
        pl.BlockSpec((1, LANES), lambda b, h, qi: (0, 0)),
    ]
    args = [q_src, uc, uc, q_src, lam_vecs, sub_g]
    if latent:
        table = pl.BlockSpec((SEQ, LANES), lambda b, h, qi: (0, 0))
        in_specs += [pl.BlockSpec((SEQ, LANES), col(8)), pl.BlockSpec((SEQ, LANES), col(16)),
                     table, table, table]
        args += [u, u, tables["ca"], tables["saa"], tables["sab"]]
    return pl.pallas_call(
        functools.partial(_mixer_a_kernel, latent=latent, lam_init=lam_init, tq=tq,
                          n_group=n_group),
        grid=(BATCH, A_HEADS, tiles),
        in_specs=in_specs,
        out_specs=pl.BlockSpec((tq, LANES), lambda b, h, qi: (b * tiles + qi, h)),
        out_shape=jax.ShapeDtypeStruct((rows, A_HEADS * HEAD_DIM), BF16),
        scratch_shapes=[pltpu.VMEM((n_keys, LANES), BF16)] * 2
        + [pltpu.VMEM((n_keys, 2 * LANES), BF16)] + _attend_scratch(tq, n_group),
        compiler_params=_params(3),
        name="mixer_a" if latent else "mixer_a_ctx",
    )(*args)


def _mixer_b_kernel(*refs, latent, tq):
    if latent:
        (sink_ref, q_ref, kc_ref, vc_ref, ug_ref, kl_ref, vl_ref, tc_ref, ts_ref, y_ref,
         ks_ref) = refs
    else:
        sink_ref, q_ref, kc_ref, vc_ref, ug_ref, y_ref = refs
    kvh = pl.program_id(1)
    qi = pl.program_id(2)
    group = B_HEADS // B_KV_HEADS
    span = tq + 2 * WINDOW

    if latent:
        @pl.when(qi == 0)
        def _():
            def prep(rows, i):
                ks_ref[rows, :] = _rope_half(kl_ref[rows, :].astype(F32), tc_ref[rows, :],
                                             ts_ref[rows, :]).astype(BF16)
            _for_row_blocks(SEQ, PREP_ROWS, prep)

        rows = pl.ds(pl.multiple_of(qi * tq, tq), tq)
        start = pl.multiple_of(jnp.clip(qi * tq - WINDOW, 0, SEQ - span), WINDOW)
        k_win = ks_ref[pl.ds(start, span), :]
        v_win = vl_ref[pl.ds(start, span), :]
        q_pos = qi * tq + lax.broadcasted_iota(jnp.int32, (tq, span), 0)
        k_pos = start + lax.broadcasted_iota(jnp.int32, (tq, span), 1)
        in_window = jnp.abs(k_pos - q_pos) <= WINDOW

    kc = kc_ref[...]
    vc = vc_ref[...]
    trans_b = (((1,), (1,)), ((), ()))
    for gi in range(group):
        cols = slice(gi * HEAD_DIM, (gi + 1) * HEAD_DIM)
        q = q_ref[:, cols].astype(F32)
        if latent:
            q = _rope_half(q, tc_ref[rows, :], ts_ref[rows, :])
        q = (q * (HEAD_DIM ** -0.5 * LOG2E)).astype(BF16)
        sink = sink_ref[kvh * group + gi] * LOG2E
        s_c = lax.dot_general(q, kc, trans_b, preferred_element_type=F32)
        m = jnp.maximum(jnp.max(s_c, axis=1, keepdims=True), sink)
        if latent:
            s_w = lax.dot_general(q, k_win, trans_b, preferred_element_type=F32)
            s_w = jnp.where(in_window, s_w, NEG_INF)
            m = jnp.maximum(m, jnp.max(s_w, axis=1, keepdims=True))
        p_c = jnp.exp2(s_c - m)
        den = jnp.sum(p_c, axis=1, keepdims=True) + jnp.exp2(sink - m)
        o = jnp.dot(p_c.astype(BF16), vc, preferred_element_type=F32)
        if latent:
            p_w = jnp.exp2(s_w - m)
            den = den + jnp.sum(p_w, axis=1, keepdims=True)
            o = o + jnp.dot(p_w.astype(BF16), v_win, preferred_element_type=F32)
        o = o * (1.0 / den)
        y_ref[:, cols] = (o * _silu(ug_ref[:, cols].astype(F32))).astype(BF16)


def _mixer_b(u, uc, sink, tables, latent):
    tq = 256
    q_src = u if latent else uc
    rows = q_src.shape[0]
    tiles = rows // BATCH // tq
    width = (B_HEADS // B_KV_HEADS) * HEAD_DIM
    q0, k0, v0 = A_IN // width, (A_IN + 1024) // LANES, (A_IN + 1280) // LANES
    gate0 = (A_IN + B_IN + 1024) // width
    col = lambda off: (lambda b, h, qi: (b, off + h))
    in_specs = [
        pl.BlockSpec(memory_space=pltpu.SMEM),
        pl.BlockSpec((tq, width), lambda b, h, qi: (b * tiles + qi, q0 + h)),
        pl.BlockSpec((CTX_LEN, LANES), col(k0)),
        pl.BlockSpec((CTX_LEN, LANES), col(v0)),
        pl.BlockSpec((tq, width), lambda b, h, qi: (b * tiles + qi, gate0 + h)),
    ]
    args = [sink, q_src, uc, uc, q_src]
    scratch = []
    if latent:
        table = pl.BlockSpec((SEQ, LANES), lambda b, h, qi: (0, 0))
        in_specs += [pl.BlockSpec((SEQ, LANES), col(k0)), pl.BlockSpec((SEQ, LANES), col(v0)),
                     table, table]
        args += [u, u, tables["c128"], tables["s128"]]
        scratch = [pltpu.VMEM((SEQ, LANES), BF16)]
    return pl.pallas_call(
        functools.partial(_mixer_b_kernel, latent=latent, tq=tq),
        grid=(BATCH, B_KV_HEADS, tiles),
        in_specs=in_specs,
        out_specs=pl.BlockSpec((tq, width), lambda b, h, qi: (b * tiles + qi, h)),
        out_shape=jax.ShapeDtypeStruct((rows, B_HEADS * HEAD_DIM), BF16),
        scratch_shapes=scratch,
        compiler_params=_params(3),
        name="mixer_b" if latent else "mixer_b_ctx",
    )(*args)


def _mixer_c_kernel(*refs, latent, tq, n_group):
    if latent:
        (q_ref, kc_ref, vc_ref, ug_ref, qg_ref, kg_ref, kl_ref, vl_ref, tc_ref, ts_ref, y_ref,
         ks_ref, vs_ref, *att) = refs
    else:
        q_ref, kc_ref, vc_ref, ug_ref, qg_ref, kg_ref, y_ref, ks_ref, vs_ref, *att = refs
    qi = pl.program_id(2)
    group = C_HEADS // C_KV_HEADS

    @pl.when(qi == 0)
    def _():
        ks_ref[0:CTX_LEN, :] = _rms(kc_ref[...].astype(F32), kg_ref[...]).astype(BF16)
        _store_values(vs_ref, slice(0, CTX_LEN), vc_ref[...])
        if latent:
            def prep(rows, i):
                k = _rms(kl_ref[rows, :].astype(F32), kg_ref[...])
                k = _rope_half(k, tc_ref[rows, :], ts_ref[rows, :])
                dst = pl.ds(pl.multiple_of(CTX_LEN + i * PREP_ROWS, CTX_LEN), PREP_ROWS)
                ks_ref[dst, :] = k.astype(BF16)
                _store_values(vs_ref, dst, vl_ref[rows, :])
            _for_row_blocks(SEQ, PREP_ROWS, prep)

    if latent:
        rows = pl.ds(pl.multiple_of(qi * tq, tq), tq)
    qs = []
    for gi in range(group):
        q = _rms(q_ref[:, gi * HEAD_DIM:(gi + 1) * HEAD_DIM].astype(F32), qg_ref[...])
        if latent:
            q = _rope_half(q, tc_ref[rows, :], ts_ref[rows, :])
        qs.append((q * (HEAD_DIM ** -0.5 * LOG2E)).astype(BF16))
    o = _attend(jnp.concatenate(qs, axis=0), ks_ref, vs_ref, att, n_group)
    for gi in range(group):
        cols = slice(gi * HEAD_DIM, (gi + 1) * HEAD_DIM)
        y_ref[:, cols] = (o[gi * tq:(gi + 1) * tq] * _silu(ug_ref[:, cols].astype(F32))).astype(BF16)


def _mixer_c(u, uc, q_g, k_g, tables, latent):
    tq = 128 if latent else CTX_LEN
    q_src = u if latent else uc
    rows = q_src.shape[0]
    tiles = rows // BATCH // tq
    group = C_HEADS // C_KV_HEADS
    width = group * HEAD_DIM
    n_group = N_KEY_GROUP if latent else 0
    n_keys = KEY_CHUNK + n_group * KEY_GROUP
    k0, v0, gate0 = 1024 // LANES, 1280 // LANES, 2048 // width
    col = lambda off: (lambda b, h, qi: (b, off + h))
    in_specs = [
        pl.BlockSpec((tq, width), lambda b, h, qi: (b * tiles + qi, h)),
        pl.BlockSpec((CTX_LEN, LANES), col(k0)),
        pl.BlockSpec((CTX_LEN, LANES), col(v0)),
        pl.BlockSpec((tq, width), lambda b, h, qi: (b * tiles + qi, gate0 + h)),
        pl.BlockSpec((1, LANES), lambda b, h, qi: (0, 0)),
        pl.BlockSpec((1, LANES), lambda b, h, qi: (0, 0)),
    ]
    args = [q_src, uc, uc, q_src, q_g, k_g]
    if latent:
        table = pl.BlockSpec((SEQ, LANES), lambda b, h, qi: (0, 0))
        in_specs += [pl.BlockSpec((SEQ, LANES), col(k0)), pl.BlockSpec((SEQ, LANES), col(v0)),
                     table, table]
        args += [u, u, tables["c128"], tables["s128"]]
    return pl.pallas_call(
        functools.partial(_mixer_c_kernel, latent=latent, tq=tq, n_group=n_group),
        grid=(BATCH, C_KV_HEADS, tiles),
        in_specs=in_specs,
        out_specs=pl.BlockSpec((tq, width), lambda b, h, qi: (b * tiles + qi, h)),
        out_shape=jax.ShapeDtypeStruct((rows, C_HEADS * HEAD_DIM), BF16),
        scratch_shapes=[pltpu.VMEM((n_keys, LANES), BF16), pltpu.VMEM((n_keys, 2 * LANES), BF16)]
        + _attend_scratch(group * tq, n_group),
        compiler_params=_params(3),
        name="mixer_c" if latent else "mixer_c_ctx",
    )(*args)


def _mixer_d_kernel(*refs, latent, tq, n_group):
    if latent:
        (q_ref, knc_ref, vc_ref, pec_ref, ug_ref, knl_ref, vl_ref, pel_ref, tc_ref, tsa_ref,
         tsb_ref, y_ref, ks_ref, vs_ref, *att) = refs
    else:
        q_ref, knc_ref, vc_ref, pec_ref, ug_ref, y_ref, ks_ref, vs_ref, *att = refs
    qi = pl.program_id(2)

    @pl.when(qi == 0)
    def _():
        ks_ref[0:CTX_LEN, 0:D_NOPE] = knc_ref[...]
        ks_ref[0:CTX_LEN, D_NOPE:2 * D_NOPE] = pec_ref[...]
        _store_values(vs_ref, slice(0, CTX_LEN), vc_ref[...])
        if latent:
            def prep(rows, i):
                pe = _rope_quarter(pel_ref[rows, :].astype(F32), tc_ref[rows, :], tsa_ref[rows, :],
                                   tsb_ref[rows, :])
                dst = pl.ds(pl.multiple_of(CTX_LEN + i * PREP_ROWS, CTX_LEN), PREP_ROWS)
                ks_ref[dst, 0:D_NOPE] = knl_ref[rows, :]
                ks_ref[dst, D_NOPE:2 * D_NOPE] = pe.astype(BF16)
                _store_values(vs_ref, dst, vl_ref[rows, :])
            _for_row_blocks(SEQ, PREP_ROWS, prep)

    scale = (D_NOPE + D_ROPE) ** -0.5 * LOG2E
    q_nope = q_ref[:, 0:D_NOPE].astype(F32)
    q_pe = q_ref[:, D_NOPE:2 * D_NOPE].astype(F32)
    if latent:
        rows = pl.ds(pl.multiple_of(qi * tq, tq), tq)
        q_pe = _rope_quarter(q_pe, tc_ref[rows, :], tsa_ref[rows, :], tsb_ref[rows, :])
    q = jnp.concatenate([(q_nope * scale).astype(BF16), (q_pe * scale).astype(BF16)], axis=1)
    o = _attend(q, ks_ref, vs_ref, att, n_group)
    y_ref[...] = (o * _silu(ug_ref[...].astype(F32))).astype(BF16)


def _mixer_d(u, uc, qd, kvd, qdc, kvdc, tables, latent):
    tq = 512 if latent else CTX_LEN
    q_src, ug_src = (qd, u) if latent else (qdc, uc)
    rows = q_src.shape[0]
    tiles = rows // BATCH // tq
    n_group = N_KEY_GROUP if latent else 0
    n_keys = KEY_CHUNK + n_group * KEY_GROUP
    pe0, gate0 = 4352 // LANES, (2048 + 1024) // LANES
    in_specs = [
        pl.BlockSpec((tq, 2 * LANES), lambda b, h, qi: (b * tiles + qi, h)),
        pl.BlockSpec((CTX_LEN, LANES), lambda b, h, qi: (b, 2 * h)),
        pl.BlockSpec((CTX_LEN, LANES), lambda b, h, qi: (b, 2 * h + 1)),
        pl.BlockSpec((CTX_LEN, LANES), lambda b, h, qi: (b, pe0)),
        pl.BlockSpec((tq, LANES), lambda b, h, qi: (b * tiles + qi, gate0 + h)),
    ]
    args = [q_src, kvdc, kvdc, uc, ug_src]
    if latent:
        table = pl.BlockSpec((SEQ, LANES), lambda b, h, qi: (0, 0))
        in_specs += [
            pl.BlockSpec((SEQ, LANES), lambda b, h, qi: (b, 2 * h)),
            pl.BlockSpec((SEQ, LANES), lambda b, h, qi: (b, 2 * h + 1)),
            pl.BlockSpec((SEQ, LANES), lambda b, h, qi: (b, pe0)),
            table, table, table]
        args += [kvd, kvd, u, tables["cd"], tables["sda"], tables["sdb"]]
    return pl.pallas_call(
        functools.partial(_mixer_d_kernel, latent=latent, tq=tq, n_group=n_group),
        grid=(BATCH, D_HEADS, tiles),
        in_specs=in_specs,
        out_specs=pl.BlockSpec((tq, LANES), lambda b, h, qi: (b * tiles + qi, h)),
        out_shape=jax.ShapeDtypeStruct((rows, D_HEADS * HEAD_DIM), BF16),
        scratch_shapes=[pltpu.VMEM((n_keys, 2 * LANES), BF16)] * 2 + _attend_scratch(tq, n_group),
        compiler_params=_params(3),
        name="mixer_d" if latent else "mixer_d_ctx",
    )(*args)


def _rope_tables():
    n = jnp.arange(SEQ)
    row = (n // GRID_W).astype(F32)
    col = (n % GRID_W).astype(F32)

    def cos_sin(dim):
        n_freq = dim // 4
        inv = ROPE_BASE ** (-jnp.arange(n_freq, dtype=F32) / n_freq)
        ang = jnp.concatenate([row[:, None] * inv[None], col[:, None] * inv[None]], axis=-1)
        return jnp.cos(ang), jnp.sin(ang)

    c, s = cos_sin(HEAD_DIM)
    tables = {"c128": jnp.concatenate([c, c], 1), "s128": jnp.concatenate([-s, s], 1)}
    c, s = cos_sin(A_QK_DIM)
    z = jnp.zeros_like(c)
    tables["ca"] = jnp.concatenate([c, c, c, c], 1)
    tables["saa"] = jnp.concatenate([-s, z, -s, z], 1)
    tables["sab"] = jnp.concatenate([z, s, z, s], 1)
    tables["cd"] = jnp.concatenate([c, c, z, z], 1)
    tables["sda"] = jnp.concatenate([-s, z, z, z], 1)
    tables["sdb"] = jnp.concatenate([z, s, z, z], 1)
    return tables


def _odd_weight(w):
    d0 = C_IN
    gate0 = C_IN + D_IN
    pieces = [w[:, :d0], w[:, d0:d0 + D_Q_RANK], w[:, gate0:],
              w[:, d0 + D_Q_RANK:d0 + D_Q_RANK + D_KV_RANK], w[:, d0 + D_Q_RANK + D_KV_RANK:gate0]]
    used = sum(p.shape[1] for p in pieces)
    pieces.append(jnp.zeros((D_MODEL, ODD_PAD - used), w.dtype))
    return jnp.concatenate(pieces, axis=1).astype(BF16)


def kernel(x, c, ctx, c_ctx, w_mod, b_mod, norm_g, w_o, final_g, e_w_in, a_lam_q1, a_lam_k1, a_lam_q2, a_lam_k2, a_sub_g, b_sink, o_w_in, c_q_g, c_k_g, d_q_a_g, d_kv_a_g, d_w_q_b, d_w_kv_b):
    xl = x.reshape(N_LAT_ROWS, D_MODEL)
    xc = ctx.reshape(N_CTX_ROWS, D_MODEL)
    cc = jnp.concatenate([c, c_ctx[None], jnp.zeros((8 - BATCH - 1, D_MODEL), F32)], axis=0)
    mod = _modulation(cc, w_mod, b_mod)
    tables = _rope_tables()

    for l in range(DEPTH):
        with_ctx = l < DEPTH - 1
        i = l // 2
        shift, scale, gate = (mod[l, :BATCH, k * D_MODEL:(k + 1) * D_MODEL][:, None, :]
                              for k in range(3))
        cshift, cscale, cgate = (mod[l, BATCH:BATCH + 1, k * D_MODEL:(k + 1) * D_MODEL][:, None, :]
                                 for k in range(3))
        g = norm_g[l][None]
        wo = w_o[l].astype(BF16)
        if l % 2 == 0:
            w_in = e_w_in[i].astype(BF16)
            u = _in_proj(xl, scale, shift, g, w_in, "in_proj_even")
            uc = _in_proj(xc, cscale, cshift, g, w_in, "in_proj_even_ctx")
            lam_init = 0.8 - 0.6 * math.exp(-0.3 * l)
            lam_vecs = jnp.stack([a_lam_q1[i], a_lam_k1[i], a_lam_q2[i], a_lam_k2[i]]).astype(F32)
            sub_g = a_sub_g[i][None]
            y1 = _mixer_a(u, uc, lam_vecs, sub_g, tables, lam_init, True)
            y2 = _mixer_b(u, uc, b_sink[i], tables, True)
            if with_ctx:
                yc1 = _mixer_a(u, uc, lam_vecs, sub_g, tables, lam_init, False)
                yc2 = _mixer_b(u, uc, b_sink[i], tables, False)
        else:
            w_in = _odd_weight(o_w_in[i])
            u = _in_proj(xl, scale, shift, g, w_in, "in_proj_odd")
            uc = _in_proj(xc, cscale, cshift, g, w_in, "in_proj_odd_ctx")
            wq = jnp.pad(d_w_q_b[i].reshape(D_Q_RANK, D_HEADS, D_NOPE + D_ROPE),
                         ((0, 0), (0, 0), (0, 2 * LANES - D_NOPE - D_ROPE)))
            wq = wq.reshape(D_Q_RANK, D_HEADS * 2 * LANES).astype(BF16)
            wkv = d_w_kv_b[i].astype(BF16)
            qd, kvd = _dproj(u, d_q_a_g[i][None], d_kv_a_g[i][None], wq, wkv, "dproj")
            qdc, kvdc = _dproj(uc, d_q_a_g[i][None], d_kv_a_g[i][None], wq, wkv, "dproj_ctx")
            y1 = _mixer_c(u, uc, c_q_g[i][None], c_k_g[i][None], tables, True)
            y2 = _mixer_d(u, uc, qd, kvd, qdc, kvdc, tables, True)
            if with_ctx:
                yc1 = _mixer_c(u, uc, c_q_g[i][None], c_k_g[i][None], tables, False)
                yc2 = _mixer_d(u, uc, qd, kvd, qdc, kvdc, tables, False)
        last = l == DEPTH - 1
        xl = _out_proj(xl, y1, y2, wo, gate, final_g[None] if last else None,
                       "out_proj_final" if last else "out_proj")
        if with_ctx:
            xc = _out_proj(xc, yc1, yc2, wo, cgate, None, "out_proj_ctx")
    return xl.reshape(BATCH, SEQ, D_MODEL)
```

```python
import functools
import math

import jax
import jax.numpy as jnp
from jax import lax
from jax.experimental import pallas as pl
from jax.experimental.pallas import tpu as pltpu

F32 = jnp.float32
BF16 = jnp.bfloat16

D_MODEL = 2048
BATCH = 2
SEQ = 4096
DEPTH = 4
GRID_W = 64
CTX_LEN = 256
HEAD_DIM = 128
ROPE_BASE = 10000.0
EPS = 1e-6
NEG_INF = -1e30
WINDOW = 128

A_HEADS = 8
A_QK_DIM = 64
B_HEADS = 8
B_KV_HEADS = 2
C_HEADS = 8
C_KV_HEADS = 2
D_HEADS = 8
D_Q_RANK = 512
D_KV_RANK = 256
D_NOPE = 128
D_ROPE = 64

A_IN = 3072
B_IN = 1536
C_IN = 1536
D_IN = 832
MIX_WIDTH = 2048
EVEN_IN = A_IN + B_IN + MIX_WIDTH
ODD_IN = C_IN + D_IN + MIX_WIDTH
ODD_PAD = 4608

LOG2E = math.log2(math.e)
LANES = 128
KEY_CHUNK = 256
KEY_GROUP = 1024
N_KEY_GROUP = SEQ // KEY_GROUP
N_LAT_ROWS = BATCH * SEQ
N_CTX_ROWS = BATCH * CTX_LEN
VMEM_LIMIT = 48 * 1024 * 1024


def _params(n_grid_dims, vmem=VMEM_LIMIT):
    return pltpu.CompilerParams(
        dimension_semantics=("arbitrary",) * n_grid_dims, vmem_limit_bytes=vmem)


def _silu(v):
    return v * (1.0 / (1.0 + jnp.exp(-v)))


def _rms(v, g):
    return v * lax.rsqrt(jnp.mean(v * v, axis=-1, keepdims=True) + EPS) * g


def _mod_kernel(cc_ref, w_ref, b_ref, o_ref):
    s = _silu(cc_ref[...]).astype(BF16)
    o_ref[0] = jnp.dot(s, w_ref[0].astype(BF16), preferred_element_type=F32) + b_ref[0]


def _modulation(cc, w_mod, b_mod):
    tn = 1024
    return pl.pallas_call(
        _mod_kernel,
        grid=(DEPTH, 3 * D_MODEL // tn),
        in_specs=[
            pl.BlockSpec((8, D_MODEL), lambda l, j: (0, 0)),
            pl.BlockSpec((1, D_MODEL, tn), lambda l, j: (l, 0, j)),
            pl.BlockSpec((1, 1, tn), lambda l, j: (l, 0, j)),
        ],
        out_specs=pl.BlockSpec((1, 8, tn), lambda l, j: (l, 0, j)),
        out_shape=jax.ShapeDtypeStruct((DEPTH, 8, 3 * D_MODEL), F32),
        compiler_params=_params(2),
        name="modulation",
    )(cc, w_mod, b_mod.reshape(DEPTH, 1, 3 * D_MODEL))


def _in_proj_kernel(x_ref, sc_ref, sh_ref, g_ref, w_ref, o_ref, h_ref):
    @pl.when(pl.program_id(1) == 0)
    def _():
        h = _rms(x_ref[...], g_ref[...]) * (1.0 + sc_ref[0]) + sh_ref[0]
        h_ref[...] = h.astype(BF16)

    o_ref[...] = jnp.dot(h_ref[...], w_ref[...], preferred_element_type=F32).astype(BF16)


def _in_proj(x, scale, shift, g, w, name):
    rows, n_out = x.shape[0], w.shape[1]
    tm = 512
    tn = n_out // (4 if n_out == EVEN_IN else 3)
    tiles_per_mod = rows // tm // scale.shape[0]
    return pl.pallas_call(
        _in_proj_kernel,
        grid=(rows // tm, n_out // tn),
        in_specs=[
            pl.BlockSpec((tm, D_MODEL), lambda i, j: (i, 0)),
            pl.BlockSpec((1, 1, D_MODEL), lambda i, j: (i // tiles_per_mod, 0, 0)),
            pl.BlockSpec((1, 1, D_MODEL), lambda i, j: (i // tiles_per_mod, 0, 0)),
            pl.BlockSpec((1, D_MODEL), lambda i, j: (0, 0)),
            pl.BlockSpec((D_MODEL, tn), lambda i, j: (0, j)),
        ],
        out_specs=pl.BlockSpec((tm, tn), lambda i, j: (i, j)),
        out_shape=jax.ShapeDtypeStruct((rows, n_out), BF16),
        scratch_shapes=[pltpu.VMEM((tm, D_MODEL), BF16)],
        compiler_params=_params(2),
        name=name,
    )(x, scale, shift, g, w)


def _out_proj_kernel(*refs, final):
    if final:
        x_ref, y1_ref, y2_ref, w_ref, gate_ref, fg_ref, o_ref = refs
    else:
        x_ref, y1_ref, y2_ref, w_ref, gate_ref, o_ref = refs
    half = MIX_WIDTH // 2
    d = jnp.dot(y1_ref[...], w_ref[0:half, :], preferred_element_type=F32)
    d = d + jnp.dot(y2_ref[...], w_ref[half:MIX_WIDTH, :], preferred_element_type=F32)
    xn = x_ref[...] + gate_ref[0] * d
    if final:
        xn = _rms(xn, fg_ref[...])
    o_ref[...] = xn


def _out_proj(x, y1, y2, w, gate, final_g, name):
    rows = x.shape[0]
    tm = 512
    tiles_per_mod = rows // tm // gate.shape[0]
    half = MIX_WIDTH // 2
    in_specs = [
        pl.BlockSpec((tm, D_MODEL), lambda i: (i, 0)),
        pl.BlockSpec((tm, half), lambda i: (i, 0)),
        pl.BlockSpec((tm, half), lambda i: (i, 0)),
        pl.BlockSpec((MIX_WIDTH, D_MODEL), lambda i: (0, 0)),
        pl.BlockSpec((1, 1, D_MODEL), lambda i: (i // tiles_per_mod, 0, 0)),
    ]
    args = [x, y1, y2, w, gate]
    if final_g is not None:
        in_specs.append(pl.BlockSpec((1, D_MODEL), lambda i: (0, 0)))
        args.append(final_g)
    return pl.pallas_call(
        functools.partial(_out_proj_kernel, final=final_g is not None),
        grid=(rows // tm,),
        in_specs=in_specs,
        out_specs=pl.BlockSpec((tm, D_MODEL), lambda i: (i, 0)),
        out_shape=jax.ShapeDtypeStruct((rows, D_MODEL), F32),
        compiler_params=_params(1, 56 * 1024 * 1024),
        name=name,
    )(*args)


def _dproj_kernel(cq_ref, ckv_ref, qg_ref, kvg_ref, wq_ref, wkv_ref, q_out, kv_out):
    cq = _rms(cq_ref[...].astype(F32), qg_ref[...]).astype(BF16)
    ckv = _rms(ckv_ref[...].astype(F32), kvg_ref[...]).astype(BF16)
    q_out[...] = jnp.dot(cq, wq_ref[...], preferred_element_type=F32).astype(BF16)
    kv_out[...] = jnp.dot(ckv, wkv_ref[...], preferred_element_type=F32).astype(BF16)


def _dproj(u, q_g, kv_g, wq, wkv, name):
    rows = u.shape[0]
    tm = 512
    n_q, n_kv = wq.shape[1], wkv.shape[1]
    return pl.pallas_call(
        _dproj_kernel,
        grid=(rows // tm,),
        in_specs=[
            pl.BlockSpec((tm, D_Q_RANK), lambda i: (i, 1536 // D_Q_RANK)),
            pl.BlockSpec((tm, D_KV_RANK), lambda i: (i, 4096 // D_KV_RANK)),
            pl.BlockSpec((1, D_Q_RANK), lambda i: (0, 0)),
            pl.BlockSpec((1, D_KV_RANK), lambda i: (0, 0)),
            pl.BlockSpec((D_Q_RANK, n_q), lambda i: (0, 0)),
            pl.BlockSpec((D_KV_RANK, n_kv), lambda i: (0, 0)),
        ],
        out_specs=[
            pl.BlockSpec((tm, n_q), lambda i: (i, 0)),
            pl.BlockSpec((tm, n_kv), lambda i: (i, 0)),
        ],
        out_shape=[
            jax.ShapeDtypeStruct((rows, n_q), BF16),
            jax.ShapeDtypeStruct((rows, n_kv), BF16),
        ],
        compiler_params=_params(1),
        name=name,
    )(u, u, q_g, kv_g, wq, wkv)


def _rope_half(x, c, s):
    return x * c + pltpu.roll(x, 64, 1) * s


def _rope_quarter(x, c, sa, sb):
    return x * c + pltpu.roll(x, 96, 1) * sa + pltpu.roll(x, 32, 1) * sb


def _attend(q, k_ref, v_ref, att, n_group):
    sc_ref, mb_ref = att[0], att[1]
    sl_ref = att[2] if n_group else None
    half = q.shape[0] // 2
    parts = [slice(0, half), slice(half, 2 * half)]
    trans_b = (((1,), (1,)), ((), ()))

    def lane_fold_max(s):
        parts = [s[:, j * LANES:(j + 1) * LANES] for j in range(s.shape[1] // LANES)]
        while len(parts) > 1:
            odd = [parts[-1]] if len(parts) % 2 else []
            parts = [jnp.maximum(a, b) for a, b in zip(parts[0::2], parts[1::2])] + odd
        return parts[0]

    def group_rows(g):
        return slice(KEY_CHUNK + g * KEY_GROUP, KEY_CHUNK + (g + 1) * KEY_GROUP)

    ms = []
    for r in parts:
        s = lax.dot_general(q[r], k_ref[0:KEY_CHUNK, :], trans_b, preferred_element_type=F32)
        sc_ref[r, :] = s
        ms.append(lane_fold_max(s))
    for g in range(n_group):
        for i, r in enumerate(parts):
            s = lax.dot_general(q[r], k_ref[group_rows(g), :], trans_b,
                                preferred_element_type=F32)
            sl_ref[g, r, :] = s
            ms[i] = jnp.maximum(ms[i], lane_fold_max(s))
    for m, r in zip(ms, parts):
        mb_ref[r, :] = jnp.broadcast_to(jnp.max(m, axis=1, keepdims=True), (half, LANES))

    def weights(s, mb):
        cols = [jnp.exp2(s[:, j * LANES:(j + 1) * LANES] - mb) for j in range(s.shape[1] // LANES)]
        return jnp.concatenate(cols, axis=1).astype(BF16)

    accs = [jnp.dot(weights(sc_ref[r, :], mb_ref[r, :]), v_ref[0:KEY_CHUNK, :],
                    preferred_element_type=F32) for r in parts]
    for g in range(n_group):
        for i, r in enumerate(parts):
            accs[i] = accs[i] + jnp.dot(weights(sl_ref[g, r, :], mb_ref[r, :]),
                                        v_ref[group_rows(g), :], preferred_element_type=F32)
    return jnp.concatenate([a[:, :LANES] * (1.0 / a[:, LANES:]) for a in accs], axis=0)


def _attend_scratch(m_rows, n_group):
    scratch = [pltpu.VMEM((m_rows, KEY_CHUNK), F32), pltpu.VMEM((m_rows, LANES), F32)]
    if n_group:
        scratch.append(pltpu.VMEM((n_group, m_rows, KEY_GROUP), F32))
    return scratch


def _store_values(vs_ref, rows, v):
    vs_ref[rows, 0:LANES] = v
    vs_ref[rows, LANES:2 * LANES] = jnp.ones(v.shape, BF16)


def _for_row_blocks(n_rows, block, fn):
    def body(i, carry):
        fn(pl.ds(pl.multiple_of(i * block, block), block), i)
        return carry
    lax.fori_loop(0, n_rows // block, body, 0)


PREP_ROWS = 512


def _mixer_a_kernel(*refs, latent, lam_init, tq, n_group):
    if latent:
        (q_ref, kc_ref, vc_ref, ug_ref, lam_ref, subg_ref, kl_ref, vl_ref, tc_ref, tsa_ref,
         tsb_ref, y_ref, ks_ref, vs_ref, *att) = refs
    else:
        q_ref, kc_ref, vc_ref, ug_ref, lam_ref, subg_ref, y_ref, ks_ref, vs_ref, *att = refs
    qi = pl.program_id(2)

    @pl.when(qi == 0)
    def _():
        ks_ref[0:CTX_LEN, :] = kc_ref[...]
        _store_values(vs_ref, slice(0, CTX_LEN), vc_ref[...])
        if latent:
            def prep(rows, i):
                k = _rope_quarter(kl_ref[rows, :].astype(F32), tc_ref[rows, :], tsa_ref[rows, :],
                                  tsb_ref[rows, :])
                dst = pl.ds(pl.multiple_of(CTX_LEN + i * PREP_ROWS, CTX_LEN), PREP_ROWS)
                ks_ref[dst, :] = k.astype(BF16)
                _store_values(vs_ref, dst, vl_ref[rows, :])
            _for_row_blocks(SEQ, PREP_ROWS, prep)

    q = q_ref[...].astype(F32)
    if latent:
        rows = pl.ds(pl.multiple_of(qi * tq, tq), tq)
        q = _rope_quarter(q, tc_ref[rows, :], tsa_ref[rows, :], tsb_ref[rows, :])
    q = q * (A_QK_DIM ** -0.5 * LOG2E)
    lane = lax.broadcasted_iota(jnp.int32, (1, LANES), 1)
    lo = (lane < A_QK_DIM).astype(F32)
    qs = jnp.concatenate([(q * lo).astype(BF16), (q * (1.0 - lo)).astype(BF16)], axis=0)
    o = _attend(qs, ks_ref, vs_ref, att, n_group)
    lv = lam_ref[...]
    lam = (jnp.exp(jnp.sum(lv[0:1] * lv[1:2], axis=1, keepdims=True))
           - jnp.exp(jnp.sum(lv[2:3] * lv[3:4], axis=1, keepdims=True)) + lam_init)
    o = o[0:tq] - lam * o[tq:2 * tq]
    y = _rms(o, subg_ref[...]) * (1.0 - lam_init)
    y_ref[...] = (y * _silu(ug_ref[...].astype(F32))).astype(BF16)


def _mixer_a(u, uc, lam_vecs, sub_g, tables, lam_init, latent):
    tq = 512 if latent else CTX_LEN
    q_src = u if latent else uc
    rows = q_src.shape[0]
    tiles = rows // BATCH // tq
    n_group = N_KEY_GROUP if latent else 0
    n_keys = KEY_CHUNK + n_group * KEY_GROUP
    gate0 = (A_IN + B_IN) // LANES
    col = lambda off: (lambda b, h, qi: (b, off + h))
    in_specs = [
        pl.BlockSpec((tq, LANES), lambda b, h, qi: (b * tiles + qi, h)),
        pl.BlockSpec((CTX_LEN, LANES), col(8)),
        pl.BlockSpec((CTX_LEN, LANES), col(16)),
        pl.BlockSpec((tq, LANES), lambda b, h, qi: (b * tiles + qi, gate0 + h)),
        pl.BlockSpec((4, A_QK_DIM), lambda b, h, qi: (0, 0)),
        pl.BlockSpec((1, LANES), lambda b, h, qi: (0, 0)),
    ]
    args = [q_src, uc, uc, q_src, lam_vecs, sub_g]
    if latent:
        table = pl.BlockSpec((SEQ, LANES), lambda b, h, qi: (0, 0))
        in_specs += [pl.BlockSpec((SEQ, LANES), col(8)), pl.BlockSpec((SEQ, LANES), col(16)),
                     table, table, table]
        args += [u, u, tables["ca"], tables["saa"], tables["sab"]]
    return pl.pallas_call(
        functools.partial(_mixer_a_kernel, latent=latent, lam_init=lam_init, tq=tq,
                          n_group=n_group),
        grid=(BATCH, A_HEADS, tiles),
        in_specs=in_specs,
        out_specs=pl.BlockSpec((tq, LANES), lambda b, h, qi: (b * tiles + qi, h)),
        out_shape=jax.ShapeDtypeStruct((rows, A_HEADS * HEAD_DIM), BF16),
        scratch_shapes=[pltpu.VMEM((n_keys, LANES), BF16), pltpu.VMEM((n_keys, 2 * LANES), BF16)]
        + _attend_scratch(2 * tq, n_group),
        compiler_params=_params(3),
        name="mixer_a" if latent else "mixer_a_ctx",
    )(*args)


def _mixer_b_kernel(*refs, latent, tq):
    if latent:
        (sink_ref, q_ref, kc_ref, vc_ref, ug_ref, kl_ref, vl_ref, tc_ref, ts_ref, y_ref,
         ks_ref) = refs
    else:
        sink_ref, q_ref, kc_ref, vc_ref, ug_ref, y_ref = refs
    kvh = pl.program_id(1)
    qi = pl.program_id(2)
    group = B_HEADS // B_KV_HEADS
    span = tq + 2 * WINDOW

    if latent:
        @pl.when(qi == 0)
        def _():
            def prep(rows, i):
                ks_ref[rows, :] = _rope_half(kl_ref[rows, :].astype(F32), tc_ref[rows, :],
                                             ts_ref[rows, :]).astype(BF16)
            _for_row_blocks(SEQ, PREP_ROWS, prep)

        rows = pl.ds(pl.multiple_of(qi * tq, tq), tq)
        start = pl.multiple_of(jnp.clip(qi * tq - WINDOW, 0, SEQ - span), WINDOW)
        k_win = ks_ref[pl.ds(start, span), :]
        v_win = vl_ref[pl.ds(start, span), :]
        q_pos = qi * tq + lax.broadcasted_iota(jnp.int32, (tq, span), 0)
        k_pos = start + lax.broadcasted_iota(jnp.int32, (tq, span), 1)
        in_window = jnp.abs(k_pos - q_pos) <= WINDOW

    kc = kc_ref[...]
    vc = vc_ref[...]
    trans_b = (((1,), (1,)), ((), ()))
    for gi in range(group):
        cols = slice(gi * HEAD_DIM, (gi + 1) * HEAD_DIM)
        q = q_ref[:, cols].astype(F32)
        if latent:
            q = _rope_half(q, tc_ref[rows, :], ts_ref[rows, :])
        q = (q * (HEAD_DIM ** -0.5 * LOG2E)).astype(BF16)
        sink = sink_ref[kvh * group + gi] * LOG2E
        s_c = lax.dot_general(q, kc, trans_b, preferred_element_type=F32)
        m = jnp.maximum(jnp.max(s_c, axis=1, keepdims=True), sink)
        if latent:
            s_w = lax.dot_general(q, k_win, trans_b, preferred_element_type=F32)
            s_w = jnp.where(in_window, s_w, NEG_INF)
            m = jnp.maximum(m, jnp.max(s_w, axis=1, keepdims=True))
        p_c = jnp.exp2(s_c - m)
        den = jnp.sum(p_c, axis=1, keepdims=True) + jnp.exp2(sink - m)
        o = jnp.dot(p_c.astype(BF16), vc, preferred_element_type=F32)
        if latent:
            p_w = jnp.exp2(s_w - m)
            den = den + jnp.sum(p_w, axis=1, keepdims=True)
            o = o + jnp.dot(p_w.astype(BF16), v_win, preferred_element_type=F32)
        o = o * (1.0 / den)
        y_ref[:, cols] = (o * _silu(ug_ref[:, cols].astype(F32))).astype(BF16)


def _mixer_b(u, uc, sink, tables, latent):
    tq = 256
    q_src = u if latent else uc
    rows = q_src.shape[0]
    tiles = rows // BATCH // tq
    width = (B_HEADS // B_KV_HEADS) * HEAD_DIM
    q0, k0, v0 = A_IN // width, (A_IN + 1024) // LANES, (A_IN + 1280) // LANES
    gate0 = (A_IN + B_IN + 1024) // width
    col = lambda off: (lambda b, h, qi: (b, off + h))
    in_specs = [
        pl.BlockSpec(memory_space=pltpu.SMEM),
        pl.BlockSpec((tq, width), lambda b, h, qi: (b * tiles + qi, q0 + h)),
        pl.BlockSpec((CTX_LEN, LANES), col(k0)),
        pl.BlockSpec((CTX_LEN, LANES), col(v0)),
        pl.BlockSpec((tq, width), lambda b, h, qi: (b * tiles + qi, gate0 + h)),
    ]
    args = [sink, q_src, uc, uc, q_src]
    scratch = []
    if latent:
        table = pl.BlockSpec((SEQ, LANES), lambda b, h, qi: (0, 0))
        in_specs += [pl.BlockSpec((SEQ, LANES), col(k0)), pl.BlockSpec((SEQ, LANES), col(v0)),
                     table, table]
        args += [u, u, tables["c128"], tables["s128"]]
        scratch = [pltpu.VMEM((SEQ, LANES), BF16)]
    return pl.pallas_call(
        functools.partial(_mixer_b_kernel, latent=latent, tq=tq),
        grid=(BATCH, B_KV_HEADS, tiles),
        in_specs=in_specs,
        out_specs=pl.BlockSpec((tq, width), lambda b, h, qi: (b * tiles + qi, h)),
        out_shape=jax.ShapeDtypeStruct((rows, B_HEADS * HEAD_DIM), BF16),
        scratch_shapes=scratch,
        compiler_params=_params(3),
        name="mixer_b" if latent else "mixer_b_ctx",
    )(*args)


def _mixer_c_kernel(*refs, latent, tq, n_group):
    if latent:
        (q_ref, kc_ref, vc_ref, ug_ref, qg_ref, kg_ref, kl_ref, vl_ref, tc_ref, ts_ref, y_ref,
         ks_ref, vs_ref, *att) = refs
    else:
        q_ref, kc_ref, vc_ref, ug_ref, qg_ref, kg_ref, y_ref, ks_ref, vs_ref, *att = refs
    qi = pl.program_id(2)
    group = C_HEADS // C_KV_HEADS

    @pl.when(qi == 0)
    def _():
        ks_ref[0:CTX_LEN, :] = _rms(kc_ref[...].astype(F32), kg_ref[...]).astype(BF16)
        _store_values(vs_ref, slice(0, CTX_LEN), vc_ref[...])
        if latent:
            def prep(rows, i):
                k = _rms(kl_ref[rows, :].astype(F32), kg_ref[...])
                k = _rope_half(k, tc_ref[rows, :], ts_ref[rows, :])
                dst = pl.ds(pl.multiple_of(CTX_LEN + i * PREP_ROWS, CTX_LEN), PREP_ROWS)
                ks_ref[dst, :] = k.astype(BF16)
                _store_values(vs_ref, dst, vl_ref[rows, :])
            _for_row_blocks(SEQ, PREP_ROWS, prep)

    if latent:
        rows = pl.ds(pl.multiple_of(qi * tq, tq), tq)
    qs = []
    for gi in range(group):
        q = _rms(q_ref[:, gi * HEAD_DIM:(gi + 1) * HEAD_DIM].astype(F32), qg_ref[...])
        if latent:
            q = _rope_half(q, tc_ref[rows, :], ts_ref[rows, :])
        qs.append((q * (HEAD_DIM ** -0.5 * LOG2E)).astype(BF16))
    o = _attend(jnp.concatenate(qs, axis=0), ks_ref, vs_ref, att, n_group)
    for gi in range(group):
        cols = slice(gi * HEAD_DIM, (gi + 1) * HEAD_DIM)
        y_ref[:, cols] = (o[gi * tq:(gi + 1) * tq] * _silu(ug_ref[:, cols].astype(F32))).astype(BF16)


def _mixer_c(u, uc, q_g, k_g, tables, latent):
    tq = CTX_LEN
    q_src = u if latent else uc
    rows = q_src.shape[0]
    tiles = rows // BATCH // tq
    group = C_HEADS // C_KV_HEADS
    width = group * HEAD_DIM
    n_group = N_KEY_GROUP if latent else 0
    n_keys = KEY_CHUNK + n_group * KEY_GROUP
    k0, v0, gate0 = 1024 // LANES, 1280 // LANES, 2048 // width
    col = lambda off: (lambda b, h, qi: (b, off + h))
    in_specs = [
        pl.BlockSpec((tq, width), lambda b, h, qi: (b * tiles + qi, h)),
        pl.BlockSpec((CTX_LEN, LANES), col(k0)),
        pl.BlockSpec((CTX_LEN, LANES), col(v0)),
        pl.BlockSpec((tq, width), lambda b, h, qi: (b * tiles + qi, gate0 + h)),
        pl.BlockSpec((1, LANES), lambda b, h, qi: (0, 0)),
        pl.BlockSpec((1, LANES), lambda b, h, qi: (0, 0)),
    ]
    args = [q_src, uc, uc, q_src, q_g, k_g]
    if latent:
        table = pl.BlockSpec((SEQ, LANES), lambda b, h, qi: (0, 0))
        in_specs += [pl.BlockSpec((SEQ, LANES), col(k0)), pl.BlockSpec((SEQ, LANES), col(v0)),
                     table, table]
        args += [u, u, tables["c128"], tables["s128"]]
    return pl.pallas_call(
        functools.partial(_mixer_c_kernel, latent=latent, tq=tq, n_group=n_group),
        grid=(BATCH, C_KV_HEADS, tiles),
        in_specs=in_specs,
        out_specs=pl.BlockSpec((tq, width), lambda b, h, qi: (b * tiles + qi, h)),
        out_shape=jax.ShapeDtypeStruct((rows, C_HEADS * HEAD_DIM), BF16),
        scratch_shapes=[pltpu.VMEM((n_keys, LANES), BF16), pltpu.VMEM((n_keys, 2 * LANES), BF16)]
        + _attend_scratch(group * tq, n_group),
        compiler_params=_params(3),
        name="mixer_c" if latent else "mixer_c_ctx",
    )(*args)


def _mixer_d_kernel(*refs, latent, tq, n_group):
    if latent:
        (q_ref, knc_ref, vc_ref, pec_ref, ug_ref, knl_ref, vl_ref, pel_ref, tc_ref, tsa_ref,
         tsb_ref, y_ref, ks_ref, vs_ref, *att) = refs
    else:
        q_ref, knc_ref, vc_ref, pec_ref, ug_ref, y_ref, ks_ref, vs_ref, *att = refs
    qi = pl.program_id(2)

    @pl.when(qi == 0)
    def _():
        ks_ref[0:CTX_LEN, 0:D_NOPE] = knc_ref[...]
        ks_ref[0:CTX_LEN, D_NOPE:2 * D_NOPE] = pec_ref[...]
        _store_values(vs_ref, slice(0, CTX_LEN), vc_ref[...])
        if latent:
            def prep(rows, i):
                pe = _rope_quarter(pel_ref[rows, :].astype(F32), tc_ref[rows, :], tsa_ref[rows, :],
                                   tsb_ref[rows, :])
                dst = pl.ds(pl.multiple_of(CTX_LEN + i * PREP_ROWS, CTX_LEN), PREP_ROWS)
                ks_ref[dst, 0:D_NOPE] = knl_ref[rows, :]
                ks_ref[dst, D_NOPE:2 * D_NOPE] = pe.astype(BF16)
                _store_values(vs_ref, dst, vl_ref[rows, :])
            _for_row_blocks(SEQ, PREP_ROWS, prep)

    scale = (D_NOPE + D_ROPE) ** -0.5 * LOG2E
    q_nope = q_ref[:, 0:D_NOPE].astype(F32)
    q_pe = q_ref[:, D_NOPE:2 * D_NOPE].astype(F32)
    if latent:
        rows = pl.ds(pl.multiple_of(qi * tq, tq), tq)
        q_pe = _rope_quarter(q_pe, tc_ref[rows, :], tsa_ref[rows, :], tsb_ref[rows, :])
    q = jnp.concatenate([(q_nope * scale).astype(BF16), (q_pe * scale).astype(BF16)], axis=1)
    o = _attend(q, ks_ref, vs_ref, att, n_group)
    y_ref[...] = (o * _silu(ug_ref[...].astype(F32))).astype(BF16)


def _mixer_d(u, uc, qd, kvd, qdc, kvdc, tables, latent):
    tq = 1024 if latent else CTX_LEN
    q_src, ug_src = (qd, u) if latent else (qdc, uc)
    rows = q_src.shape[0]
    tiles = rows // BATCH // tq
    n_group = N_KEY_GROUP if latent else 0
    n_keys = KEY_CHUNK + n_group * KEY_GROUP
    pe0, gate0 = 4352 // LANES, (2048 + 1024) // LANES
    in_specs = [
        pl.BlockSpec((tq, 2 * LANES), lambda b, h, qi: (b * tiles + qi, h)),
        pl.BlockSpec((CTX_LEN, LANES), lambda b, h, qi: (b, 2 * h)),
        pl.BlockSpec((CTX_LEN, LANES), lambda b, h, qi: (b, 2 * h + 1)),
        pl.BlockSpec((CTX_LEN, LANES), lambda b, h, qi: (b, pe0)),
        pl.BlockSpec((tq, LANES), lambda b, h, qi: (b * tiles + qi, gate0 + h)),
    ]
    args = [q_src, kvdc, kvdc, uc, ug_src]
    if latent:
        table = pl.BlockSpec((SEQ, LANES), lambda b, h, qi: (0, 0))
        in_specs += [
            pl.BlockSpec((SEQ, LANES), lambda b, h, qi: (b, 2 * h)),
            pl.BlockSpec((SEQ, LANES), lambda b, h, qi: (b, 2 * h + 1)),
            pl.BlockSpec((SEQ, LANES), lambda b, h, qi: (b, pe0)),
            table, table, table]
        args += [kvd, kvd, u, tables["cd"], tables["sda"], tables["sdb"]]
    return pl.pallas_call(
        functools.partial(_mixer_d_kernel, latent=latent, tq=tq, n_group=n_group),
        grid=(BATCH, D_HEADS, tiles),
        in_specs=in_specs,
        out_specs=pl.BlockSpec((tq, LANES), lambda b, h, qi: (b * tiles + qi, h)),
        out_shape=jax.ShapeDtypeStruct((rows, D_HEADS * HEAD_DIM), BF16),
        scratch_shapes=[pltpu.VMEM((n_keys, 2 * LANES), BF16)] * 2 + _attend_scratch(tq, n_group),
        compiler_params=_params(3),
        name="mixer_d" if latent else "mixer_d_ctx",
    )(*args)


def _rope_tables():
    n = jnp.arange(SEQ)
    row = (n // GRID_W).astype(F32)
    col = (n % GRID_W).astype(F32)

    def cos_sin(dim):
        n_freq = dim // 4
        inv = ROPE_BASE ** (-jnp.arange(n_freq, dtype=F32) / n_freq)
        ang = jnp.concatenate([row[:, None] * inv[None], col[:, None] * inv[None]], axis=-1)
        return jnp.cos(ang), jnp.sin(ang)

    c, s = cos_sin(HEAD_DIM)
    tables = {"c128": jnp.concatenate([c, c], 1), "s128": jnp.concatenate([-s, s], 1)}
    c, s = cos_sin(A_QK_DIM)
    z = jnp.zeros_like(c)
    tables["ca"] = jnp.concatenate([c, c, c, c], 1)
    tables["saa"] = jnp.concatenate([-s, z, -s, z], 1)
    tables["sab"] = jnp.concatenate([z, s, z, s], 1)
    tables["cd"] = jnp.concatenate([c, c, z, z], 1)
    tables["sda"] = jnp.concatenate([-s, z, z, z], 1)
    tables["sdb"] = jnp.concatenate([z, s, z, z], 1)
    return tables


def _odd_weight(w):
    w = w.astype(BF16)
    d0 = C_IN
    gate0 = C_IN + D_IN
    pieces = [w[:, :d0 + D_Q_RANK], w[:, gate0:], w[:, d0 + D_Q_RANK:gate0]]
    used = sum(p.shape[1] for p in pieces)
    pieces.append(jnp.zeros((D_MODEL, ODD_PAD - used), BF16))
    return jnp.concatenate(pieces, axis=1)


def kernel(x, c, ctx, c_ctx, w_mod, b_mod, norm_g, w_o, final_g, e_w_in, a_lam_q1, a_lam_k1, a_lam_q2, a_lam_k2, a_sub_g, b_sink, o_w_in, c_q_g, c_k_g, d_q_a_g, d_kv_a_g, d_w_q_b, d_w_kv_b):
    xl = x.reshape(N_LAT_ROWS, D_MODEL)
    xc = ctx.reshape(N_CTX_ROWS, D_MODEL)
    cc = jnp.concatenate([c, c_ctx[None], jnp.zeros((8 - BATCH - 1, D_MODEL), F32)], axis=0)
    mod = _modulation(cc, w_mod, b_mod)
    tables = _rope_tables()

    for l in range(DEPTH):
        with_ctx = l < DEPTH - 1
        i = l // 2
        shift, scale, gate = (mod[l, :BATCH, k * D_MODEL:(k + 1) * D_MODEL][:, None, :]
                              for k in range(3))
        cshift, cscale, cgate = (mod[l, BATCH:BATCH + 1, k * D_MODEL:(k + 1) * D_MODEL][:, None, :]
                                 for k in range(3))
        g = norm_g[l][None]
        wo = w_o[l].astype(BF16)
        if l % 2 == 0:
            w_in = e_w_in[i].astype(BF16)
            u = _in_proj(xl, scale, shift, g, w_in, "in_proj_even")
            uc = _in_proj(xc, cscale, cshift, g, w_in, "in_proj_even_ctx")
            lam_init = 0.8 - 0.6 * math.exp(-0.3 * l)
            lam_vecs = jnp.stack([a_lam_q1[i], a_lam_k1[i], a_lam_q2[i], a_lam_k2[i]]).astype(F32)
            sub_g = a_sub_g[i][None]
            y1 = _mixer_a(u, uc, lam_vecs, sub_g, tables, lam_init, True)
            y2 = _mixer_b(u, uc, b_sink[i], tables, True)
            if with_ctx:
                yc1 = _mixer_a(u, uc, lam_vecs, sub_g, tables, lam_init, False)
                yc2 = _mixer_b(u, uc, b_sink[i], tables, False)
        else:
            w_in = _odd_weight(o_w_in[i])
            u = _in_proj(xl, scale, shift, g, w_in, "in_proj_odd")
            uc = _in_proj(xc, cscale, cshift, g, w_in, "in_proj_odd_ctx")
            wq = jnp.pad(d_w_q_b[i].reshape(D_Q_RANK, D_HEADS, D_NOPE + D_ROPE),
                         ((0, 0), (0, 0), (0, 2 * LANES - D_NOPE - D_ROPE)))
            wq = wq.reshape(D_Q_RANK, D_HEADS * 2 * LANES).astype(BF16)
            wkv = d_w_kv_b[i].astype(BF16)
            qd, kvd = _dproj(u, d_q_a_g[i][None], d_kv_a_g[i][None], wq, wkv, "dproj")
            qdc, kvdc = _dproj(uc, d_q_a_g[i][None], d_kv_a_g[i][None], wq, wkv, "dproj_ctx")
            y1 = _mixer_c(u, uc, c_q_g[i][None], c_k_g[i][None], tables, True)
            y2 = _mixer_d(u, uc, qd, kvd, qdc, kvdc, tables, True)
            if with_ctx:
                yc1 = _mixer_c(u, uc, c_q_g[i][None], c_k_g[i][None], tables, False)
                yc2 = _mixer_d(u, uc, qd, kvd, qdc, kvdc, tables, False)
        last = l == DEPTH - 1
        xl = _out_proj(xl, y1, y2, wo, gate, final_g[None] if last else None,
                       "out_proj_final" if last else "out_proj")
        if with_ctx:
            xc = _out_proj(xc, yc1, yc2, wo, cgate, None, "out_proj_ctx")
    return xl.reshape(BATCH, SEQ, D_MODEL)
```

```python
import functools
import math

import jax
import jax.numpy as jnp
from jax import lax
from jax.experimental import pallas as pl
from jax.experimental.pallas import tpu as pltpu

F32 = jnp.float32
BF16 = jnp.bfloat16

D_MODEL = 2048
BATCH = 2
SEQ = 4096
DEPTH = 4
GRID_W = 64
CTX_LEN = 256
HEAD_DIM = 128
ROPE_BASE = 10000.0
EPS = 1e-6
NEG_INF = -1e30
WINDOW = 128

A_HEADS = 8
A_QK_DIM = 64
B_HEADS = 8
B_KV_HEADS = 2
C_HEADS = 8
C_KV_HEADS = 2
D_HEADS = 8
D_Q_RANK = 512
D_KV_RANK = 256
D_NOPE = 128
D_ROPE = 64

A_IN = 3072
B_IN = 1536
C_IN = 1536
D_IN = 832
MIX_WIDTH = 2048
EVEN_IN = A_IN + B_IN + MIX_WIDTH
ODD_IN = C_IN + D_IN + MIX_WIDTH
ODD_PAD = 4608

LOG2E = math.log2(math.e)
LANES = 128
KEY_CHUNK = 256
KEY_GROUP = 1024
N_KEY_GROUP = SEQ // KEY_GROUP
N_LAT_ROWS = BATCH * SEQ
N_CTX_ROWS = BATCH * CTX_LEN
VMEM_LIMIT = 48 * 1024 * 1024


def _params(n_grid_dims, vmem=VMEM_LIMIT):
    return pltpu.CompilerParams(
        dimension_semantics=("arbitrary",) * n_grid_dims, vmem_limit_bytes=vmem)


def _silu(v):
    return v * (1.0 / (1.0 + jnp.exp(-v)))


def _rms(v, g):
    return v * lax.rsqrt(jnp.mean(v * v, axis=-1, keepdims=True) + EPS) * g


def _mod_kernel(cc_ref, w_ref, b_ref, o_ref):
    s = _silu(cc_ref[...]).astype(BF16)
    o_ref[0] = jnp.dot(s, w_ref[0].astype(BF16), preferred_element_type=F32) + b_ref[0]


def _modulation(cc, w_mod, b_mod):
    tn = 1024
    return pl.pallas_call(
        _mod_kernel,
        grid=(DEPTH, 3 * D_MODEL // tn),
        in_specs=[
            pl.BlockSpec((8, D_MODEL), lambda l, j: (0, 0)),
            pl.BlockSpec((1, D_MODEL, tn), lambda l, j: (l, 0, j)),
            pl.BlockSpec((1, 1, tn), lambda l, j: (l, 0, j)),
        ],
        out_specs=pl.BlockSpec((1, 8, tn), lambda l, j: (l, 0, j)),
        out_shape=jax.ShapeDtypeStruct((DEPTH, 8, 3 * D_MODEL), F32),
        compiler_params=_params(2),
        name="modulation",
    )(cc, w_mod, b_mod.reshape(DEPTH, 1, 3 * D_MODEL))


def _in_proj_kernel(x0_ref, sc0_ref, sh0_ref, xn_ref, scn_ref, shn_ref, g_ref, w_ref, o_ref,
                    ha_ref, hb_ref, *, n_col_tiles):
    i = pl.program_id(0)
    j = pl.program_id(1)

    def normed(x_ref, sc_ref, sh_ref):
        return (_rms(x_ref[...], g_ref[...]) * (1.0 + sc_ref[0]) + sh_ref[0]).astype(BF16)

    @pl.when((i == 0) & (j == 0))
    def _():
        ha_ref[...] = normed(x0_ref, sc0_ref, sh0_ref)

    for parity, (cur_ref, nxt_ref) in enumerate(((ha_ref, hb_ref), (hb_ref, ha_ref))):
        @pl.when((i % 2 == parity) & (j < n_col_tiles - 1))
        def _():
            o_ref[...] = jnp.dot(cur_ref[...], w_ref[...], preferred_element_type=F32).astype(BF16)

        @pl.when((i % 2 == parity) & (j == n_col_tiles - 1))
        def _():
            nxt_ref[...] = normed(xn_ref, scn_ref, shn_ref)
            o_ref[...] = jnp.dot(cur_ref[...], w_ref[...], preferred_element_type=F32).astype(BF16)


def _in_proj(x, scale, shift, g, w, layer, name):
    rows, n_out = x.shape[0], w.shape[2]
    tm = 512
    tn = n_out // (4 if n_out == EVEN_IN else 3)
    n_row_tiles = rows // tm
    tiles_per_mod = n_row_tiles // scale.shape[0]
    nxt = lambda i: jnp.minimum(i + 1, n_row_tiles - 1)
    mod_spec = lambda row_tile: pl.BlockSpec(
        (1, 1, D_MODEL), lambda i, j: (row_tile(i) // tiles_per_mod, 0, 0))
    first = lambda i: 0
    return pl.pallas_call(
        functools.partial(_in_proj_kernel, n_col_tiles=n_out // tn),
        grid=(n_row_tiles, n_out // tn),
        in_specs=[
            pl.BlockSpec((tm, D_MODEL), lambda i, j: (0, 0)),
            mod_spec(first),
            mod_spec(first),
            pl.BlockSpec((tm, D_MODEL), lambda i, j: (nxt(i), 0)),
            mod_spec(nxt),
            mod_spec(nxt),
            pl.BlockSpec((1, D_MODEL), lambda i, j: (0, 0)),
            pl.BlockSpec((None, D_MODEL, tn), lambda i, j: (layer, 0, j)),
        ],
        out_specs=pl.BlockSpec((tm, tn), lambda i, j: (i, j)),
        out_shape=jax.ShapeDtypeStruct((rows, n_out), BF16),
        scratch_shapes=[pltpu.VMEM((tm, D_MODEL), BF16)] * 2,
        compiler_params=_params(2),
        name=name,
    )(x, scale, shift, x, scale, shift, g, w)


def _out_proj_kernel(*refs, final):
    if final:
        x_ref, y1_ref, y2_ref, w_ref, gate_ref, fg_ref, o_ref = refs
    else:
        x_ref, y1_ref, y2_ref, w_ref, gate_ref, o_ref = refs
    half = MIX_WIDTH // 2
    d = jnp.dot(y1_ref[...], w_ref[0:half, :], preferred_element_type=F32)
    d = d + jnp.dot(y2_ref[...], w_ref[half:MIX_WIDTH, :], preferred_element_type=F32)
    xn = x_ref[...] + gate_ref[0] * d
    if final:
        xn = _rms(xn, fg_ref[...])
    o_ref[...] = xn


def _out_proj(x, y1, y2, w, layer, gate, final_g, name):
    rows = x.shape[0]
    tm = 512
    tiles_per_mod = rows // tm // gate.shape[0]
    half = MIX_WIDTH // 2
    in_specs = [
        pl.BlockSpec((tm, D_MODEL), lambda i: (i, 0)),
        pl.BlockSpec((tm, half), lambda i: (i, 0)),
        pl.BlockSpec((tm, half), lambda i: (i, 0)),
        pl.BlockSpec((None, MIX_WIDTH, D_MODEL), lambda i: (layer, 0, 0)),
        pl.BlockSpec((1, 1, D_MODEL), lambda i: (i // tiles_per_mod, 0, 0)),
    ]
    args = [x, y1, y2, w, gate]
    if final_g is not None:
        in_specs.append(pl.BlockSpec((1, D_MODEL), lambda i: (0, 0)))
        args.append(final_g)
    return pl.pallas_call(
        functools.partial(_out_proj_kernel, final=final_g is not None),
        grid=(rows // tm,),
        in_specs=in_specs,
        out_specs=pl.BlockSpec((tm, D_MODEL), lambda i: (i, 0)),
        out_shape=jax.ShapeDtypeStruct((rows, D_MODEL), F32),
        compiler_params=_params(1, 56 * 1024 * 1024),
        name=name,
    )(*args)


def _dproj_kernel(cq_ref, ckv_ref, qg_ref, kvg_ref, wq_ref, wkv_ref, q_out, kv_out):
    cq = _rms(cq_ref[...].astype(F32), qg_ref[...]).astype(BF16)
    ckv = _rms(ckv_ref[...].astype(F32), kvg_ref[...]).astype(BF16)
    q_out[...] = jnp.dot(cq, wq_ref[...], preferred_element_type=F32).astype(BF16)
    kv_out[...] = jnp.dot(ckv, wkv_ref[...], preferred_element_type=F32).astype(BF16)


def _dproj(u, q_g, kv_g, wq, wkv, name):
    rows = u.shape[0]
    tm = 512
    n_q, n_kv = wq.shape[1], wkv.shape[1]
    return pl.pallas_call(
        _dproj_kernel,
        grid=(rows // tm,),
        in_specs=[
            pl.BlockSpec((tm, D_Q_RANK), lambda i: (i, 1536 // D_Q_RANK)),
            pl.BlockSpec((tm, D_KV_RANK), lambda i: (i, 4096 // D_KV_RANK)),
            pl.BlockSpec((1, D_Q_RANK), lambda i: (0, 0)),
            pl.BlockSpec((1, D_KV_RANK), lambda i: (0, 0)),
            pl.BlockSpec((D_Q_RANK, n_q), lambda i: (0, 0)),
            pl.BlockSpec((D_KV_RANK, n_kv), lambda i: (0, 0)),
        ],
        out_specs=[
            pl.BlockSpec((tm, n_q), lambda i: (i, 0)),
            pl.BlockSpec((tm, n_kv), lambda i: (i, 0)),
        ],
        out_shape=[
            jax.ShapeDtypeStruct((rows, n_q), BF16),
            jax.ShapeDtypeStruct((rows, n_kv), BF16),
        ],
        compiler_params=_params(1),
        name=name,
    )(u, u, q_g, kv_g, wq, wkv)


def _rope_half(x, c, s):
    return x * c + pltpu.roll(x, 64, 1) * s


def _rope_quarter(x, c, sa, sb):
    return x * c + pltpu.roll(x, 96, 1) * sa + pltpu.roll(x, 32, 1) * sb


def _attend(q, k_ref, v_ref, att, n_group):
    sc_ref, mb_ref = att[0], att[1]
    sl_ref = att[2] if n_group else None
    half = q.shape[0] // 2
    parts = [slice(0, half), slice(half, 2 * half)]
    trans_b = (((1,), (1,)), ((), ()))

    def lane_fold_max(s):
        parts = [s[:, j * LANES:(j + 1) * LANES] for j in range(s.shape[1] // LANES)]
        while len(parts) > 1:
            odd = [parts[-1]] if len(parts) % 2 else []
            parts = [jnp.maximum(a, b) for a, b in zip(parts[0::2], parts[1::2])] + odd
        return parts[0]

    def group_rows(g):
        return slice(KEY_CHUNK + g * KEY_GROUP, KEY_CHUNK + (g + 1) * KEY_GROUP)

    ms = []
    for r in parts:
        s = lax.dot_general(q[r], k_ref[0:KEY_CHUNK, :], trans_b, preferred_element_type=F32)
        sc_ref[r, :] = s
        ms.append(lane_fold_max(s))
    for g in range(n_group):
        for i, r in enumerate(parts):
            s = lax.dot_general(q[r], k_ref[group_rows(g), :], trans_b,
                                preferred_element_type=F32)
            sl_ref[g, r, :] = s
            ms[i] = jnp.maximum(ms[i], lane_fold_max(s))
    for m, r in zip(ms, parts):
        mb_ref[r, :] = jnp.broadcast_to(jnp.max(m, axis=1, keepdims=True), (half, LANES))

    def weights(s, mb):
        cols = [jnp.exp2(s[:, j * LANES:(j + 1) * LANES] - mb) for j in range(s.shape[1] // LANES)]
        return jnp.concatenate(cols, axis=1).astype(BF16)

    accs = [jnp.dot(weights(sc_ref[r, :], mb_ref[r, :]), v_ref[0:KEY_CHUNK, :],
                    preferred_element_type=F32) for r in parts]
    for g in range(n_group):
        for i, r in enumerate(parts):
            accs[i] = accs[i] + jnp.dot(weights(sl_ref[g, r, :], mb_ref[r, :]),
                                        v_ref[group_rows(g), :], preferred_element_type=F32)
    return jnp.concatenate([a[:, :LANES] * (1.0 / a[:, LANES:]) for a in accs], axis=0)


def _attend_scratch(m_rows, n_group):
    scratch = [pltpu.VMEM((m_rows, KEY_CHUNK), F32), pltpu.VMEM((m_rows, LANES), F32)]
    if n_group:
        scratch.append(pltpu.VMEM((n_group, m_rows, KEY_GROUP), F32))
    return scratch


def _store_values(vs_ref, rows, v):
    vs_ref[rows, 0:LANES] = v
    vs_ref[rows, LANES:2 * LANES] = jnp.ones(v.shape, BF16)


def _for_row_blocks(n_rows, block, fn):
    def body(i, carry):
        fn(pl.ds(pl.multiple_of(i * block, block), block), i)
        return carry
    lax.fori_loop(0, n_rows // block, body, 0)


PREP_ROWS = 512


def _next_step(b, h, qi, n_heads, tiles):
    t = jnp.minimum((b * n_heads + h) * tiles + qi + 1, BATCH * n_heads * tiles - 1)
    return t // (n_heads * tiles), (t // tiles) % n_heads, t % tiles


def _with_queries_ahead(prep, q_ref, qn_ref, qs_ref, n_heads, tiles, attend):
    b, h, qi = pl.program_id(0), pl.program_id(1), pl.program_id(2)
    step = (b * n_heads + h) * tiles + qi
    slot = step % 2

    @pl.when(step == 0)
    def _():
        qs_ref[0] = prep(q_ref, qi)

    out = attend(qs_ref[slot])
    qs_ref[1 - slot] = prep(qn_ref, _next_step(b, h, qi, n_heads, tiles)[2])
    return out


def _mixer_a_kernel(*refs, latent, lam_init, tq, tiles, n_group):
    if latent:
        (q_ref, kc_ref, vc_ref, ug_ref, lam_ref, subg_ref, kl_ref, vl_ref, tc_ref, tsa_ref,
         tsb_ref, qn_ref, y_ref, ks_ref, vs_ref, qs_ref, *att) = refs
    else:
        q_ref, kc_ref, vc_ref, ug_ref, lam_ref, subg_ref, y_ref, ks_ref, vs_ref, *att = refs
    qi = pl.program_id(2)

    @pl.when(qi == 0)
    def _():
        ks_ref[0:CTX_LEN, :] = kc_ref[...]
        _store_values(vs_ref, slice(0, CTX_LEN), vc_ref[...])
        if latent:
            def prep(rows, i):
                k = _rope_quarter(kl_ref[rows, :].astype(F32), tc_ref[rows, :], tsa_ref[rows, :],
                                  tsb_ref[rows, :])
                dst = pl.ds(pl.multiple_of(CTX_LEN + i * PREP_ROWS, CTX_LEN), PREP_ROWS)
                ks_ref[dst, :] = k.astype(BF16)
                _store_values(vs_ref, dst, vl_ref[rows, :])
            _for_row_blocks(SEQ, PREP_ROWS, prep)

    def prep(src_ref, tile):
        q = src_ref[...].astype(F32)
        if latent:
            rows = pl.ds(pl.multiple_of(tile * tq, tq), tq)
            q = _rope_quarter(q, tc_ref[rows, :], tsa_ref[rows, :], tsb_ref[rows, :])
        q = q * (A_QK_DIM ** -0.5 * LOG2E)
        lane = lax.broadcasted_iota(jnp.int32, (1, LANES), 1)
        lo = (lane < A_QK_DIM).astype(F32)
        return jnp.concatenate([(q * lo).astype(BF16), (q * (1.0 - lo)).astype(BF16)], axis=0)

    attend = lambda qs: _attend(qs, ks_ref, vs_ref, att, n_group)
    if latent:
        o = _with_queries_ahead(prep, q_ref, qn_ref, qs_ref, A_HEADS, tiles, attend)
    else:
        o = attend(prep(q_ref, 0))
    lv = lam_ref[...]
    lam = (jnp.exp(jnp.sum(lv[0:1] * lv[1:2], axis=1, keepdims=True))
           - jnp.exp(jnp.sum(lv[2:3] * lv[3:4], axis=1, keepdims=True)) + lam_init)
    o = o[0:tq] - lam * o[tq:2 * tq]
    y = _rms(o, subg_ref[...]) * (1.0 - lam_init)
    y_ref[...] = (y * _silu(ug_ref[...].astype(F32))).astype(BF16)


def _mixer_a(u, uc, lam_vecs, sub_g, tables, lam_init, latent):
    tq = 512 if latent else CTX_LEN
    q_src = u if latent else uc
    rows = q_src.shape[0]
    tiles = rows // BATCH // tq
    n_group = N_KEY_GROUP if latent else 0
    n_keys = KEY_CHUNK + n_group * KEY_GROUP
    gate0 = (A_IN + B_IN) // LANES
    col = lambda off: (lambda b, h, qi: (b, off + h))
    in_specs = [
        pl.BlockSpec((tq, LANES), lambda b, h, qi: (b * tiles + qi, h)),
        pl.BlockSpec((CTX_LEN, LANES), col(8)),
        pl.BlockSpec((CTX_LEN, LANES), col(16)),
        pl.BlockSpec((tq, LANES), lambda b, h, qi: (b * tiles + qi, gate0 + h)),
        pl.BlockSpec((4, A_QK_DIM), lambda b, h, qi: (0, 0)),
        pl.BlockSpec((1, LANES), lambda b, h, qi: (0, 0)),
    ]
    args = [q_src, uc, uc, q_src, lam_vecs, sub_g]
    if latent:
        table = pl.BlockSpec((SEQ, LANES), lambda b, h, qi: (0, 0))

        def next_q(b, h, qi):
            b, h, qi = _next_step(b, h, qi, A_HEADS, tiles)
            return b * tiles + qi, h
        in_specs += [pl.BlockSpec((SEQ, LANES), col(8)), pl.BlockSpec((SEQ, LANES), col(16)),
                     table, table, table, pl.BlockSpec((tq, LANES), next_q)]
        args += [u, u, tables["ca"], tables["saa"], tables["sab"], u]
    return pl.pallas_call(
        functools.partial(_mixer_a_kernel, latent=latent, lam_init=lam_init, tq=tq, tiles=tiles,
                          n_group=n_group),
        grid=(BATCH, A_HEADS, tiles),
        in_specs=in_specs,
        out_specs=pl.BlockSpec((tq, LANES), lambda b, h, qi: (b * tiles + qi, h)),
        out_shape=jax.ShapeDtypeStruct((rows, A_HEADS * HEAD_DIM), BF16),
        scratch_shapes=[pltpu.VMEM((n_keys, LANES), BF16), pltpu.VMEM((n_keys, 2 * LANES), BF16)]
        + ([pltpu.VMEM((2, 2 * tq, LANES), BF16)] if latent else [])
        + _attend_scratch(2 * tq, n_group),
        compiler_params=_params(3),
        name="mixer_a" if latent else "mixer_a_ctx",
    )(*args)


def _mixer_b_kernel(*refs, latent, tq):
    if latent:
        (sink_ref, q_ref, kc_ref, vc_ref, ug_ref, kl_ref, vl_ref, tc_ref, ts_ref, y_ref,
         ks_ref) = refs
    else:
        sink_ref, q_ref, kc_ref, vc_ref, ug_ref, y_ref = refs
    kvh = pl.program_id(1)
    qi = pl.program_id(2)
    group = B_HEADS // B_KV_HEADS
    span = tq + 2 * WINDOW

    if latent:
        @pl.when(qi == 0)
        def _():
            def prep(rows, i):
                ks_ref[rows, :] = _rope_half(kl_ref[rows, :].astype(F32), tc_ref[rows, :],
                                             ts_ref[rows, :]).astype(BF16)
            _for_row_blocks(SEQ, PREP_ROWS, prep)

        rows = pl.ds(pl.multiple_of(qi * tq, tq), tq)
        start = pl.multiple_of(jnp.clip(qi * tq - WINDOW, 0, SEQ - span), WINDOW)
        k_win = ks_ref[pl.ds(start, span), :]
        v_win = vl_ref[pl.ds(start, span), :]
        q_pos = qi * tq + lax.broadcasted_iota(jnp.int32, (tq, span), 0)
        k_pos = start + lax.broadcasted_iota(jnp.int32, (tq, span), 1)
        in_window = jnp.abs(k_pos - q_pos) <= WINDOW

    kc = kc_ref[...]
    vc = vc_ref[...]
    trans_b = (((1,), (1,)), ((), ()))
    for gi in range(group):
        cols = slice(gi * HEAD_DIM, (gi + 1) * HEAD_DIM)
        q = q_ref[:, cols].astype(F32)
        if latent:
            q = _rope_half(q, tc_ref[rows, :], ts_ref[rows, :])
        q = (q * (HEAD_DIM ** -0.5 * LOG2E)).astype(BF16)
        sink = sink_ref[kvh * group + gi] * LOG2E
        s_c = lax.dot_general(q, kc, trans_b, preferred_element_type=F32)
        m = jnp.maximum(jnp.max(s_c, axis=1, keepdims=True), sink)
        if latent:
            s_w = lax.dot_general(q, k_win, trans_b, preferred_element_type=F32)
            s_w = jnp.where(in_window, s_w, NEG_INF)
            m = jnp.maximum(m, jnp.max(s_w, axis=1, keepdims=True))
        p_c = jnp.exp2(s_c - m)
        den = jnp.sum(p_c, axis=1, keepdims=True) + jnp.exp2(sink - m)
        o = jnp.dot(p_c.astype(BF16), vc, preferred_element_type=F32)
        if latent:
            p_w = jnp.exp2(s_w - m)
            den = den + jnp.sum(p_w, axis=1, keepdims=True)
            o = o + jnp.dot(p_w.astype(BF16), v_win, preferred_element_type=F32)
        o = o * (1.0 / den)
        y_ref[:, cols] = (o * _silu(ug_ref[:, cols].astype(F32))).astype(BF16)


def _mixer_b(u, uc, sink, tables, latent):
    tq = 256
    q_src = u if latent else uc
    rows = q_src.shape[0]
    tiles = rows // BATCH // tq
    width = (B_HEADS // B_KV_HEADS) * HEAD_DIM
    q0, k0, v0 = A_IN // width, (A_IN + 1024) // LANES, (A_IN + 1280) // LANES
    gate0 = (A_IN + B_IN + 1024) // width
    col = lambda off: (lambda b, h, qi: (b, off + h))
    in_specs = [
        pl.BlockSpec(memory_space=pltpu.SMEM),
        pl.BlockSpec((tq, width), lambda b, h, qi: (b * tiles + qi, q0 + h)),
        pl.BlockSpec((CTX_LEN, LANES), col(k0)),
        pl.BlockSpec((CTX_LEN, LANES), col(v0)),
        pl.BlockSpec((tq, width), lambda b, h, qi: (b * tiles + qi, gate0 + h)),
    ]
    args = [sink, q_src, uc, uc, q_src]
    scratch = []
    if latent:
        table = pl.BlockSpec((SEQ, LANES), lambda b, h, qi: (0, 0))
        in_specs += [pl.BlockSpec((SEQ, LANES), col(k0)), pl.BlockSpec((SEQ, LANES), col(v0)),
                     table, table]
        args += [u, u, tables["c128"], tables["s128"]]
        scratch = [pltpu.VMEM((SEQ, LANES), BF16)]
    return pl.pallas_call(
        functools.partial(_mixer_b_kernel, latent=latent, tq=tq),
        grid=(BATCH, B_KV_HEADS, tiles),
        in_specs=in_specs,
        out_specs=pl.BlockSpec((tq, width), lambda b, h, qi: (b * tiles + qi, h)),
        out_shape=jax.ShapeDtypeStruct((rows, B_HEADS * HEAD_DIM), BF16),
        scratch_shapes=scratch,
        compiler_params=_params(3),
        name="mixer_b" if latent else "mixer_b_ctx",
    )(*args)


def _mixer_c_kernel(*refs, latent, tq, tiles, n_group):
    if latent:
        (q_ref, kc_ref, vc_ref, ug_ref, qg_ref, kg_ref, kl_ref, vl_ref, tc_ref, ts_ref, qn_ref,
         y_ref, ks_ref, vs_ref, qs_ref, *att) = refs
    else:
        q_ref, kc_ref, vc_ref, ug_ref, qg_ref, kg_ref, y_ref, ks_ref, vs_ref, *att = refs
    qi = pl.program_id(2)
    group = C_HEADS // C_KV_HEADS

    @pl.when(qi == 0)
    def _():
        ks_ref[0:CTX_LEN, :] = _rms(kc_ref[...].astype(F32), kg_ref[...]).astype(BF16)
        _store_values(vs_ref, slice(0, CTX_LEN), vc_ref[...])
        if latent:
            def prep(rows, i):
                k = _rms(kl_ref[rows, :].astype(F32), kg_ref[...])
                k = _rope_half(k, tc_ref[rows, :], ts_ref[rows, :])
                dst = pl.ds(pl.multiple_of(CTX_LEN + i * PREP_ROWS, CTX_LEN), PREP_ROWS)
                ks_ref[dst, :] = k.astype(BF16)
                _store_values(vs_ref, dst, vl_ref[rows, :])
            _for_row_blocks(SEQ, PREP_ROWS, prep)

    def prep(src_ref, tile):
        if latent:
            rows = pl.ds(pl.multiple_of(tile * tq, tq), tq)
        qs = []
        for gi in range(group):
            q = _rms(src_ref[:, gi * HEAD_DIM:(gi + 1) * HEAD_DIM].astype(F32), qg_ref[...])
            if latent:
                q = _rope_half(q, tc_ref[rows, :], ts_ref[rows, :])
            qs.append((q * (HEAD_DIM ** -0.5 * LOG2E)).astype(BF16))
        return jnp.concatenate(qs, axis=0)

    attend = lambda qs: _attend(qs, ks_ref, vs_ref, att, n_group)
    if latent:
        o = _with_queries_ahead(prep, q_ref, qn_ref, qs_ref, C_KV_HEADS, tiles, attend)
    else:
        o = attend(prep(q_ref, 0))
    for gi in range(group):
        cols = slice(gi * HEAD_DIM, (gi + 1) * HEAD_DIM)
        y_ref[:, cols] = (o[gi * tq:(gi + 1) * tq] * _silu(ug_ref[:, cols].astype(F32))).astype(BF16)


def _mixer_c(u, uc, q_g, k_g, tables, latent):
    tq = CTX_LEN
    q_src = u if latent else uc
    rows = q_src.shape[0]
    tiles = rows // BATCH // tq
    group = C_HEADS // C_KV_HEADS
    width = group * HEAD_DIM
    n_group = N_KEY_GROUP if latent else 0
    n_keys = KEY_CHUNK + n_group * KEY_GROUP
    k0, v0, gate0 = 1024 // LANES, 1280 // LANES, 2048 // width
    col = lambda off: (lambda b, h, qi: (b, off + h))
    in_specs = [
        pl.BlockSpec((tq, width), lambda b, h, qi: (b * tiles + qi, h)),
        pl.BlockSpec((CTX_LEN, LANES), col(k0)),
        pl.BlockSpec((CTX_LEN, LANES), col(v0)),
        pl.BlockSpec((tq, width), lambda b, h, qi: (b * tiles + qi, gate0 + h)),
        pl.BlockSpec((1, LANES), lambda b, h, qi: (0, 0)),
        pl.BlockSpec((1, LANES), lambda b, h, qi: (0, 0)),
    ]
    args = [q_src, uc, uc, q_src, q_g, k_g]
    if latent:
        table = pl.BlockSpec((SEQ, LANES), lambda b, h, qi: (0, 0))

        def next_q(b, h, qi):
            b, h, qi = _next_step(b, h, qi, C_KV_HEADS, tiles)
            return b * tiles + qi, h
        in_specs += [pl.BlockSpec((SEQ, LANES), col(k0)), pl.BlockSpec((SEQ, LANES), col(v0)),
                     table, table, pl.BlockSpec((tq, width), next_q)]
        args += [u, u, tables["c128"], tables["s128"], u]
    return pl.pallas_call(
        functools.partial(_mixer_c_kernel, latent=latent, tq=tq, tiles=tiles, n_group=n_group),
        grid=(BATCH, C_KV_HEADS, tiles),
        in_specs=in_specs,
        out_specs=pl.BlockSpec((tq, width), lambda b, h, qi: (b * tiles + qi, h)),
        out_shape=jax.ShapeDtypeStruct((rows, C_HEADS * HEAD_DIM), BF16),
        scratch_shapes=[pltpu.VMEM((n_keys, LANES), BF16), pltpu.VMEM((n_keys, 2 * LANES), BF16)]
        + ([pltpu.VMEM((2, group * tq, LANES), BF16)] if latent else [])
        + _attend_scratch(group * tq, n_group),
        compiler_params=_params(3),
        name="mixer_c" if latent else "mixer_c_ctx",
    )(*args)


def _mixer_d_kernel(*refs, latent, tq, tiles, n_group):
    if latent:
        (q_ref, knc_ref, vc_ref, pec_ref, ug_ref, knl_ref, vl_ref, pel_ref, tc_ref, tsa_ref,
         tsb_ref, qn_ref, y_ref, ks_ref, vs_ref, qs_ref, *att) = refs
    else:
        q_ref, knc_ref, vc_ref, pec_ref, ug_ref, y_ref, ks_ref, vs_ref, *att = refs
    qi = pl.program_id(2)

    @pl.when(qi == 0)
    def _():
        ks_ref[0:CTX_LEN, 0:D_NOPE] = knc_ref[...]
        ks_ref[0:CTX_LEN, D_NOPE:2 * D_NOPE] = pec_ref[...]
        _store_values(vs_ref, slice(0, CTX_LEN), vc_ref[...])
        if latent:
            def prep(rows, i):
                pe = _rope_quarter(pel_ref[rows, :].astype(F32), tc_ref[rows, :], tsa_ref[rows, :],
                                   tsb_ref[rows, :])
                dst = pl.ds(pl.multiple_of(CTX_LEN + i * PREP_ROWS, CTX_LEN), PREP_ROWS)
                ks_ref[dst, 0:D_NOPE] = knl_ref[rows, :]
                ks_ref[dst, D_NOPE:2 * D_NOPE] = pe.astype(BF16)
                _store_values(vs_ref, dst, vl_ref[rows, :])
            _for_row_blocks(SEQ, PREP_ROWS, prep)

    def prep(src_ref, tile):
        scale = (D_NOPE + D_ROPE) ** -0.5 * LOG2E
        q_nope = src_ref[:, 0:D_NOPE].astype(F32)
        q_pe = src_ref[:, D_NOPE:2 * D_NOPE].astype(F32)
        if latent:
            rows = pl.ds(pl.multiple_of(tile * tq, tq), tq)
            q_pe = _rope_quarter(q_pe, tc_ref[rows, :], tsa_ref[rows, :], tsb_ref[rows, :])
        return jnp.concatenate([(q_nope * scale).astype(BF16), (q_pe * scale).astype(BF16)], axis=1)

    attend = lambda q: _attend(q, ks_ref, vs_ref, att, n_group)
    if latent:
        o = _with_queries_ahead(prep, q_ref, qn_ref, qs_ref, D_HEADS, tiles, attend)
    else:
        o = attend(prep(q_ref, 0))
    y_ref[...] = (o * _silu(ug_ref[...].astype(F32))).astype(BF16)


def _mixer_d(u, uc, qd, kvd, qdc, kvdc, tables, latent):
    tq = 1024 if latent else CTX_LEN
    q_src, ug_src = (qd, u) if latent else (qdc, uc)
    rows = q_src.shape[0]
    tiles = rows // BATCH // tq
    n_group = N_KEY_GROUP if latent else 0
    n_keys = KEY_CHUNK + n_group * KEY_GROUP
    pe0, gate0 = 4352 // LANES, (2048 + 1024) // LANES
    in_specs = [
        pl.BlockSpec((tq, 2 * LANES), lambda b, h, qi: (b * tiles + qi, h)),
        pl.BlockSpec((CTX_LEN, LANES), lambda b, h, qi: (b, 2 * h)),
        pl.BlockSpec((CTX_LEN, LANES), lambda b, h, qi: (b, 2 * h + 1)),
        pl.BlockSpec((CTX_LEN, LANES), lambda b, h, qi: (b, pe0)),
        pl.BlockSpec((tq, LANES), lambda b, h, qi: (b * tiles + qi, gate0 + h)),
    ]
    args = [q_src, kvdc, kvdc, uc, ug_src]
    if latent:
        table = pl.BlockSpec((SEQ, LANES), lambda b, h, qi: (0, 0))

        def next_q(b, h, qi):
            b, h, qi = _next_step(b, h, qi, D_HEADS, tiles)
            return b * tiles + qi, h
        in_specs += [
            pl.BlockSpec((SEQ, LANES), lambda b, h, qi: (b, 2 * h)),
            pl.BlockSpec((SEQ, LANES), lambda b, h, qi: (b, 2 * h + 1)),
            pl.BlockSpec((SEQ, LANES), lambda b, h, qi: (b, pe0)),
            table, table, table, pl.BlockSpec((tq, 2 * LANES), next_q)]
        args += [kvd, kvd, u, tables["cd"], tables["sda"], tables["sdb"], qd]
    return pl.pallas_call(
        functools.partial(_mixer_d_kernel, latent=latent, tq=tq, tiles=tiles, n_group=n_group),
        grid=(BATCH, D_HEADS, tiles),
        in_specs=in_specs,
        out_specs=pl.BlockSpec((tq, LANES), lambda b, h, qi: (b * tiles + qi, h)),
        out_shape=jax.ShapeDtypeStruct((rows, D_HEADS * HEAD_DIM), BF16),
        scratch_shapes=[pltpu.VMEM((n_keys, 2 * LANES), BF16)] * 2
        + ([pltpu.VMEM((2, tq, 2 * LANES), BF16)] if latent else [])
        + _attend_scratch(tq, n_group),
        compiler_params=_params(3),
        name="mixer_d" if latent else "mixer_d_ctx",
    )(*args)


def _rope_tables():
    n = jnp.arange(SEQ)
    row = (n // GRID_W).astype(F32)
    col = (n % GRID_W).astype(F32)

    def cos_sin(dim):
        n_freq = dim // 4
        inv = ROPE_BASE ** (-jnp.arange(n_freq, dtype=F32) / n_freq)
        ang = jnp.concatenate([row[:, None] * inv[None], col[:, None] * inv[None]], axis=-1)
        return jnp.cos(ang), jnp.sin(ang)

    c, s = cos_sin(HEAD_DIM)
    tables = {"c128": jnp.concatenate([c, c], 1), "s128": jnp.concatenate([-s, s], 1)}
    c, s = cos_sin(A_QK_DIM)
    z = jnp.zeros_like(c)
    tables["ca"] = jnp.concatenate([c, c, c, c], 1)
    tables["saa"] = jnp.concatenate([-s, z, -s, z], 1)
    tables["sab"] = jnp.concatenate([z, s, z, s], 1)
    tables["cd"] = jnp.concatenate([c, c, z, z], 1)
    tables["sda"] = jnp.concatenate([-s, z, z, z], 1)
    tables["sdb"] = jnp.concatenate([z, s, z, z], 1)
    return tables


def _odd_weights(w):
    w = w.astype(BF16)
    d0 = C_IN
    gate0 = C_IN + D_IN
    pieces = [w[..., :d0 + D_Q_RANK], w[..., gate0:], w[..., d0 + D_Q_RANK:gate0]]
    used = sum(p.shape[-1] for p in pieces)
    pieces.append(jnp.zeros(w.shape[:-1] + (ODD_PAD - used,), BF16))
    return jnp.concatenate(pieces, axis=-1)


def kernel(x, c, ctx, c_ctx, w_mod, b_mod, norm_g, w_o, final_g, e_w_in, a_lam_q1, a_lam_k1, a_lam_q2, a_lam_k2, a_sub_g, b_sink, o_w_in, c_q_g, c_k_g, d_q_a_g, d_kv_a_g, d_w_q_b, d_w_kv_b):
    xl = x.reshape(N_LAT_ROWS, D_MODEL)
    xc = ctx.reshape(N_CTX_ROWS, D_MODEL)
    cc = jnp.concatenate([c, c_ctx[None], jnp.zeros((8 - BATCH - 1, D_MODEL), F32)], axis=0)
    mod = _modulation(cc, w_mod, b_mod)
    tables = _rope_tables()
    w_out = w_o.astype(BF16)
    w_even = e_w_in.astype(BF16)
    w_odd = _odd_weights(o_w_in)

    for l in range(DEPTH):
        with_ctx = l < DEPTH - 1
        i = l // 2
        shift, scale, gate = (mod[l, :BATCH, k * D_MODEL:(k + 1) * D_MODEL][:, None, :]
                              for k in range(3))
        cshift, cscale, cgate = (mod[l, BATCH:BATCH + 1, k * D_MODEL:(k + 1) * D_MODEL][:, None, :]
                                 for k in range(3))
        g = norm_g[l][None]
        if l % 2 == 0:
            u = _in_proj(xl, scale, shift, g, w_even, i, "in_proj_even")
            uc = _in_proj(xc, cscale, cshift, g, w_even, i, "in_proj_even_ctx")
            lam_init = 0.8 - 0.6 * math.exp(-0.3 * l)
            lam_vecs = jnp.stack([a_lam_q1[i], a_lam_k1[i], a_lam_q2[i], a_lam_k2[i]]).astype(F32)
            sub_g = a_sub_g[i][None]
            y1 = _mixer_a(u, uc, lam_vecs, sub_g, tables, lam_init, True)
            y2 = _mixer_b(u, uc, b_sink[i], tables, True)
            if with_ctx:
                yc1 = _mixer_a(u, uc, lam_vecs, sub_g, tables, lam_init, False)
                yc2 = _mixer_b(u, uc, b_sink[i], tables, False)
        else:
            u = _in_proj(xl, scale, shift, g, w_odd, i, "in_proj_odd")
            uc = _in_proj(xc, cscale, cshift, g, w_odd, i, "in_proj_odd_ctx")
            wq = jnp.pad(d_w_q_b[i].reshape(D_Q_RANK, D_HEADS, D_NOPE + D_ROPE),
                         ((0, 0), (0, 0), (0, 2 * LANES - D_NOPE - D_ROPE)))
            wq = wq.reshape(D_Q_RANK, D_HEADS * 2 * LANES).astype(BF16)
            wkv = d_w_kv_b[i].astype(BF16)
            qd, kvd = _dproj(u, d_q_a_g[i][None], d_kv_a_g[i][None], wq, wkv, "dproj")
            qdc, kvdc = _dproj(uc, d_q_a_g[i][None], d_kv_a_g[i][None], wq, wkv, "dproj_ctx")
            y1 = _mixer_c(u, uc, c_q_g[i][None], c_k_g[i][None], tables, True)
            y2 = _mixer_d(u, uc, qd, kvd, qdc, kvdc, tables, True)
            if with_ctx:
                yc1 = _mixer_c(u, uc, c_q_g[i][None], c_k_g[i][None], tables, False)
                yc2 = _mixer_d(u, uc, qd, kvd, qdc, kvdc, tables, False)
        last = l == DEPTH - 1
        xl = _out_proj(xl, y1, y2, w_out, l, gate, final_g[None] if last else None,
                       "out_proj_final" if last else "out_proj")
        if with_ctx:
            xc = _out_proj(xc, yc1, yc2, w_out, l, cgate, None, "out_proj_ctx")
    return xl.reshape(BATCH, SEQ, D_MODEL)
```

```python
import functools
import math

import jax
import jax.numpy as jnp
from jax import lax
from jax.experimental import pallas as pl
from jax.experimental.pallas import tpu as pltpu

F32 = jnp.float32
BF16 = jnp.bfloat16

D_MODEL = 2048
BATCH = 2
SEQ = 4096
DEPTH = 4
GRID_W = 64
CTX_LEN = 256
HEAD_DIM = 128
ROPE_BASE = 10000.0
EPS = 1e-6
NEG_INF = -1e30
WINDOW = 128

A_HEADS = 8
A_QK_DIM = 64
B_HEADS = 8
B_KV_HEADS = 2
C_HEADS = 8
C_KV_HEADS = 2
D_HEADS = 8
D_Q_RANK = 512
D_KV_RANK = 256
D_NOPE = 128
D_ROPE = 64

A_IN = 3072
B_IN = 1536
C_IN = 1536
D_IN = 832
MIX_WIDTH = 2048
EVEN_IN = A_IN + B_IN + MIX_WIDTH
ODD_IN = C_IN + D_IN + MIX_WIDTH
ODD_PAD = 4608

LOG2E = math.log2(math.e)
LANES = 128
KEY_CHUNK = 256
KEY_GROUP = 1024
N_KEY_GROUP = SEQ // KEY_GROUP
N_LAT_ROWS = BATCH * SEQ
N_CTX_ROWS = BATCH * CTX_LEN
VMEM_LIMIT = 48 * 1024 * 1024


def _params(n_grid_dims, vmem=VMEM_LIMIT):
    return pltpu.CompilerParams(
        dimension_semantics=("arbitrary",) * n_grid_dims, vmem_limit_bytes=vmem)


def _silu(v):
    return v * (1.0 / (1.0 + jnp.exp(-v)))


def _rms(v, g):
    return v * lax.rsqrt(jnp.mean(v * v, axis=-1, keepdims=True) + EPS) * g


def _mod_kernel(cc_ref, w_ref, b_ref, o_ref):
    s = _silu(cc_ref[...]).astype(BF16)
    o_ref[0] = jnp.dot(s, w_ref[0].astype(BF16), preferred_element_type=F32) + b_ref[0]


def _modulation(cc, w_mod, b_mod):
    tn = 1024
    return pl.pallas_call(
        _mod_kernel,
        grid=(DEPTH, 3 * D_MODEL // tn),
        in_specs=[
            pl.BlockSpec((8, D_MODEL), lambda l, j: (0, 0)),
            pl.BlockSpec((1, D_MODEL, tn), lambda l, j: (l, 0, j)),
            pl.BlockSpec((1, 1, tn), lambda l, j: (l, 0, j)),
        ],
        out_specs=pl.BlockSpec((1, 8, tn), lambda l, j: (l, 0, j)),
        out_shape=jax.ShapeDtypeStruct((DEPTH, 8, 3 * D_MODEL), F32),
        compiler_params=_params(2),
        name="modulation",
    )(cc, w_mod, b_mod.reshape(DEPTH, 1, 3 * D_MODEL))


def _in_proj_kernel(x0_ref, sc0_ref, sh0_ref, xn_ref, scn_ref, shn_ref, g_ref, w_ref, o_ref,
                    ha_ref, hb_ref, *, n_col_tiles):
    i = pl.program_id(0)
    j = pl.program_id(1)

    def normed(x_ref, sc_ref, sh_ref):
        return (_rms(x_ref[...], g_ref[...]) * (1.0 + sc_ref[0]) + sh_ref[0]).astype(BF16)

    @pl.when((i == 0) & (j == 0))
    def _():
        ha_ref[...] = normed(x0_ref, sc0_ref, sh0_ref)

    for parity, (cur_ref, nxt_ref) in enumerate(((ha_ref, hb_ref), (hb_ref, ha_ref))):
        @pl.when((i % 2 == parity) & (j < n_col_tiles - 1))
        def _():
            o_ref[...] = jnp.dot(cur_ref[...], w_ref[...], preferred_element_type=F32).astype(BF16)

        @pl.when((i % 2 == parity) & (j == n_col_tiles - 1))
        def _():
            nxt_ref[...] = normed(xn_ref, scn_ref, shn_ref)
            o_ref[...] = jnp.dot(cur_ref[...], w_ref[...], preferred_element_type=F32).astype(BF16)


def _in_proj(x, scale, shift, g, w, layer, name):
    rows, n_out = x.shape[0], w.shape[2]
    tm = 512
    tn = n_out // (4 if n_out == EVEN_IN else 3)
    n_row_tiles = rows // tm
    tiles_per_mod = n_row_tiles // scale.shape[0]
    nxt = lambda i: jnp.minimum(i + 1, n_row_tiles - 1)
    mod_spec = lambda row_tile: pl.BlockSpec(
        (1, 1, D_MODEL), lambda i, j: (row_tile(i) // tiles_per_mod, 0, 0))
    first = lambda i: 0
    return pl.pallas_call(
        functools.partial(_in_proj_kernel, n_col_tiles=n_out // tn),
        grid=(n_row_tiles, n_out // tn),
        in_specs=[
            pl.BlockSpec((tm, D_MODEL), lambda i, j: (0, 0)),
            mod_spec(first),
            mod_spec(first),
            pl.BlockSpec((tm, D_MODEL), lambda i, j: (nxt(i), 0)),
            mod_spec(nxt),
            mod_spec(nxt),
            pl.BlockSpec((1, D_MODEL), lambda i, j: (0, 0)),
            pl.BlockSpec((None, D_MODEL, tn), lambda i, j: (layer, 0, j)),
        ],
        out_specs=pl.BlockSpec((tm, tn), lambda i, j: (i, j)),
        out_shape=jax.ShapeDtypeStruct((rows, n_out), BF16),
        scratch_shapes=[pltpu.VMEM((tm, D_MODEL), BF16)] * 2,
        compiler_params=_params(2),
        name=name,
    )(x, scale, shift, x, scale, shift, g, w)


def _out_proj_kernel(*refs, final):
    if final:
        x_ref, y1_ref, y2_ref, w_ref, gate_ref, fg_ref, o_ref = refs
    else:
        x_ref, y1_ref, y2_ref, w_ref, gate_ref, o_ref = refs
    half = MIX_WIDTH // 2
    d = jnp.dot(y1_ref[...], w_ref[0:half, :], preferred_element_type=F32)
    d = d + jnp.dot(y2_ref[...], w_ref[half:MIX_WIDTH, :], preferred_element_type=F32)
    xn = x_ref[...] + gate_ref[0] * d
    if final:
        xn = _rms(xn, fg_ref[...])
    o_ref[...] = xn


def _out_proj(x, y1, y2, w, layer, gate, final_g, name):
    rows = x.shape[0]
    tm = 512
    tiles_per_mod = rows // tm // gate.shape[0]
    half = MIX_WIDTH // 2
    in_specs = [
        pl.BlockSpec((tm, D_MODEL), lambda i: (i, 0)),
        pl.BlockSpec((tm, half), lambda i: (i, 0)),
        pl.BlockSpec((tm, half), lambda i: (i, 0)),
        pl.BlockSpec((None, MIX_WIDTH, D_MODEL), lambda i: (layer, 0, 0)),
        pl.BlockSpec((1, 1, D_MODEL), lambda i: (i // tiles_per_mod, 0, 0)),
    ]
    args = [x, y1, y2, w, gate]
    if final_g is not None:
        in_specs.append(pl.BlockSpec((1, D_MODEL), lambda i: (0, 0)))
        args.append(final_g)
    return pl.pallas_call(
        functools.partial(_out_proj_kernel, final=final_g is not None),
        grid=(rows // tm,),
        in_specs=in_specs,
        out_specs=pl.BlockSpec((tm, D_MODEL), lambda i: (i, 0)),
        out_shape=jax.ShapeDtypeStruct((rows, D_MODEL), F32),
        compiler_params=_params(1, 56 * 1024 * 1024),
        name=name,
    )(*args)


def _dproj_kernel(cq_ref, ckv_ref, qg_ref, kvg_ref, wq_ref, wkv_ref, q_out, kv_out):
    cq = _rms(cq_ref[...].astype(F32), qg_ref[...]).astype(BF16)
    ckv = _rms(ckv_ref[...].astype(F32), kvg_ref[...]).astype(BF16)
    q_out[...] = jnp.dot(cq, wq_ref[...], preferred_element_type=F32).astype(BF16)
    kv_out[...] = jnp.dot(ckv, wkv_ref[...], preferred_element_type=F32).astype(BF16)


def _dproj(u, q_g, kv_g, wq, wkv, name):
    rows = u.shape[0]
    tm = 512
    n_q, n_kv = wq.shape[1], wkv.shape[1]
    return pl.pallas_call(
        _dproj_kernel,
        grid=(rows // tm,),
        in_specs=[
            pl.BlockSpec((tm, D_Q_RANK), lambda i: (i, 1536 // D_Q_RANK)),
            pl.BlockSpec((tm, D_KV_RANK), lambda i: (i, 4096 // D_KV_RANK)),
            pl.BlockSpec((1, D_Q_RANK), lambda i: (0, 0)),
            pl.BlockSpec((1, D_KV_RANK), lambda i: (0, 0)),
            pl.BlockSpec((D_Q_RANK, n_q), lambda i: (0, 0)),
            pl.BlockSpec((D_KV_RANK, n_kv), lambda i: (0, 0)),
        ],
        out_specs=[
            pl.BlockSpec((tm, n_q), lambda i: (i, 0)),
            pl.BlockSpec((tm, n_kv), lambda i: (i, 0)),
        ],
        out_shape=[
            jax.ShapeDtypeStruct((rows, n_q), BF16),
            jax.ShapeDtypeStruct((rows, n_kv), BF16),
        ],
        compiler_params=_params(1),
        name=name,
    )(u, u, q_g, kv_g, wq, wkv)


def _rope_half(x, c, s):
    return x * c + pltpu.roll(x, 64, 1) * s


def _rope_quarter(x, c, sa, sb):
    return x * c + pltpu.roll(x, 96, 1) * sa + pltpu.roll(x, 32, 1) * sb


def _attend(q, k_ref, v_ref, att, n_group):
    sc_ref, mb_ref = att[0], att[1]
    sl_ref = att[2] if n_group else None
    half = q.shape[0] // 2
    parts = [slice(0, half), slice(half, 2 * half)]
    trans_b = (((1,), (1,)), ((), ()))

    def lane_fold_max(s):
        parts = [s[:, j * LANES:(j + 1) * LANES] for j in range(s.shape[1] // LANES)]
        while len(parts) > 1:
            odd = [parts[-1]] if len(parts) % 2 else []
            parts = [jnp.maximum(a, b) for a, b in zip(parts[0::2], parts[1::2])] + odd
        return parts[0]

    def group_rows(g):
        return slice(KEY_CHUNK + g * KEY_GROUP, KEY_CHUNK + (g + 1) * KEY_GROUP)

    ms = []
    for r in parts:
        s = lax.dot_general(q[r], k_ref[0:KEY_CHUNK, :], trans_b, preferred_element_type=F32)
        sc_ref[r, :] = s
        ms.append(lane_fold_max(s))
    for g in range(n_group):
        for i, r in enumerate(parts):
            s = lax.dot_general(q[r], k_ref[group_rows(g), :], trans_b,
                                preferred_element_type=F32)
            sl_ref[g, r, :] = s
            ms[i] = jnp.maximum(ms[i], lane_fold_max(s))
    for m, r in zip(ms, parts):
        mb_ref[r, :] = jnp.broadcast_to(jnp.max(m, axis=1, keepdims=True), (half, LANES))

    def weights(s, mb):
        cols = [jnp.exp2(s[:, j * LANES:(j + 1) * LANES] - mb) for j in range(s.shape[1] // LANES)]
        return jnp.concatenate(cols, axis=1).astype(BF16)

    accs = [jnp.dot(weights(sc_ref[r, :], mb_ref[r, :]), v_ref[0:KEY_CHUNK, :],
                    preferred_element_type=F32) for r in parts]
    for g in range(n_group):
        for i, r in enumerate(parts):
            accs[i] = accs[i] + jnp.dot(weights(sl_ref[g, r, :], mb_ref[r, :]),
                                        v_ref[group_rows(g), :], preferred_element_type=F32)
    return jnp.concatenate([a[:, :LANES] * (1.0 / a[:, LANES:]) for a in accs], axis=0)


def _attend_scratch(m_rows, n_group):
    scratch = [pltpu.VMEM((m_rows, KEY_CHUNK), F32), pltpu.VMEM((m_rows, LANES), F32)]
    if n_group:
        scratch.append(pltpu.VMEM((n_group, m_rows, KEY_GROUP), F32))
    return scratch


def _store_values(vs_ref, rows, v):
    vs_ref[rows, 0:LANES] = v
    vs_ref[rows, LANES:2 * LANES] = jnp.ones(v.shape, BF16)


def _for_row_blocks(n_rows, block, fn):
    def body(i, carry):
        fn(pl.ds(pl.multiple_of(i * block, block), block), i)
        return carry
    lax.fori_loop(0, n_rows // block, body, 0)


PREP_ROWS = 512


def _next_step(b, h, qi, n_heads, tiles):
    t = jnp.minimum((b * n_heads + h) * tiles + qi + 1, BATCH * n_heads * tiles - 1)
    return t // (n_heads * tiles), (t // tiles) % n_heads, t % tiles


def _with_queries_ahead(prep, q_ref, qn_ref, qs_ref, n_heads, tiles, attend):
    b, h, qi = pl.program_id(0), pl.program_id(1), pl.program_id(2)
    step = (b * n_heads + h) * tiles + qi
    slot = step % 2

    @pl.when(step == 0)
    def _():
        qs_ref[0] = prep(q_ref, qi)

    out = attend(qs_ref[slot])
    qs_ref[1 - slot] = prep(qn_ref, _next_step(b, h, qi, n_heads, tiles)[2])
    return out


def _mixer_a_kernel(*refs, latent, lam_init, tq, n_group):
    if latent:
        (q_ref, kc_ref, vc_ref, ug_ref, lam_ref, subg_ref, kl_ref, vl_ref, tc_ref, tsa_ref,
         tsb_ref, y_ref, ks_ref, vs_ref, *att) = refs
    else:
        q_ref, kc_ref, vc_ref, ug_ref, lam_ref, subg_ref, y_ref, ks_ref, vs_ref, *att = refs
    qi = pl.program_id(2)

    @pl.when(qi == 0)
    def _():
        ks_ref[0:CTX_LEN, :] = kc_ref[...]
        _store_values(vs_ref, slice(0, CTX_LEN), vc_ref[...])
        if latent:
            def prep(rows, i):
                k = _rope_quarter(kl_ref[rows, :].astype(F32), tc_ref[rows, :], tsa_ref[rows, :],
                                  tsb_ref[rows, :])
                dst = pl.ds(pl.multiple_of(CTX_LEN + i * PREP_ROWS, CTX_LEN), PREP_ROWS)
                ks_ref[dst, :] = k.astype(BF16)
                _store_values(vs_ref, dst, vl_ref[rows, :])
            _for_row_blocks(SEQ, PREP_ROWS, prep)

    def prep(src_ref, tile):
        q = src_ref[...].astype(F32)
        if latent:
            rows = pl.ds(pl.multiple_of(tile * tq, tq), tq)
            q = _rope_quarter(q, tc_ref[rows, :], tsa_ref[rows, :], tsb_ref[rows, :])
        q = q * (A_QK_DIM ** -0.5 * LOG2E)
        lane = lax.broadcasted_iota(jnp.int32, (1, LANES), 1)
        lo = (lane < A_QK_DIM).astype(F32)
        return jnp.concatenate([(q * lo).astype(BF16), (q * (1.0 - lo)).astype(BF16)], axis=0)

    attend = lambda qs: _attend(qs, ks_ref, vs_ref, att, n_group)
    o = attend(prep(q_ref, qi))
    lv = lam_ref[...]
    lam = (jnp.exp(jnp.sum(lv[0:1] * lv[1:2], axis=1, keepdims=True))
           - jnp.exp(jnp.sum(lv[2:3] * lv[3:4], axis=1, keepdims=True)) + lam_init)
    o = o[0:tq] - lam * o[tq:2 * tq]
    y = _rms(o, subg_ref[...]) * (1.0 - lam_init)
    y_ref[...] = (y * _silu(ug_ref[...].astype(F32))).astype(BF16)


def _mixer_a(u, uc, lam_vecs, sub_g, tables, lam_init, latent):
    tq = 512 if latent else CTX_LEN
    q_src = u if latent else uc
    rows = q_src.shape[0]
    tiles = rows // BATCH // tq
    n_group = N_KEY_GROUP if latent else 0
    n_keys = KEY_CHUNK + n_group * KEY_GROUP
    gate0 = (A_IN + B_IN) // LANES
    col = lambda off: (lambda b, h, qi: (b, off + h))
    in_specs = [
        pl.BlockSpec((tq, LANES), lambda b, h, qi: (b * tiles + qi, h)),
        pl.BlockSpec((CTX_LEN, LANES), col(8)),
        pl.BlockSpec((CTX_LEN, LANES), col(16)),
        pl.BlockSpec((tq, LANES), lambda b, h, qi: (b * tiles + qi, gate0 + h)),
        pl.BlockSpec((4, A_QK_DIM), lambda b, h, qi: (0, 0)),
        pl.BlockSpec((1, LANES), lambda b, h, qi: (0, 0)),
    ]
    args = [q_src, uc, uc, q_src, lam_vecs, sub_g]
    if latent:
        table = pl.BlockSpec((SEQ, LANES), lambda b, h, qi: (0, 0))
        in_specs += [pl.BlockSpec((SEQ, LANES), col(8)), pl.BlockSpec((SEQ, LANES), col(16)),
                     table, table, table]
        args += [u, u, tables["ca"], tables["saa"], tables["sab"]]
    return pl.pallas_call(
        functools.partial(_mixer_a_kernel, latent=latent, lam_init=lam_init, tq=tq,
                          n_group=n_group),
        grid=(BATCH, A_HEADS, tiles),
        in_specs=in_specs,
        out_specs=pl.BlockSpec((tq, LANES), lambda b, h, qi: (b * tiles + qi, h)),
        out_shape=jax.ShapeDtypeStruct((rows, A_HEADS * HEAD_DIM), BF16),
        scratch_shapes=[pltpu.VMEM((n_keys, LANES), BF16), pltpu.VMEM((n_keys, 2 * LANES), BF16)]
        + _attend_scratch(2 * tq, n_group),
        compiler_params=_params(3),
        name="mixer_a" if latent else "mixer_a_ctx",
    )(*args)


def _mixer_b_kernel(*refs, latent, tq):
    if latent:
        (sink_ref, q_ref, kc_ref, vc_ref, ug_ref, kl_ref, vl_ref, tc_ref, ts_ref, y_ref,
         ks_ref) = refs
    else:
        sink_ref, q_ref, kc_ref, vc_ref, ug_ref, y_ref = refs
    kvh = pl.program_id(1)
    qi = pl.program_id(2)
    group = B_HEADS // B_KV_HEADS
    span = tq + 2 * WINDOW

    if latent:
        @pl.when(qi == 0)
        def _():
            def prep(rows, i):
                ks_ref[rows, :] = _rope_half(kl_ref[rows, :].astype(F32), tc_ref[rows, :],
                                             ts_ref[rows, :]).astype(BF16)
            _for_row_blocks(SEQ, PREP_ROWS, prep)

        rows = pl.ds(pl.multiple_of(qi * tq, tq), tq)
        start = pl.multiple_of(jnp.clip(qi * tq - WINDOW, 0, SEQ - span), WINDOW)
        k_win = ks_ref[pl.ds(start, span), :]
        v_win = vl_ref[pl.ds(start, span), :]
        q_pos = qi * tq + lax.broadcasted_iota(jnp.int32, (tq, span), 0)
        k_pos = start + lax.broadcasted_iota(jnp.int32, (tq, span), 1)
        in_window = jnp.abs(k_pos - q_pos) <= WINDOW

    kc = kc_ref[...]
    vc = vc_ref[...]
    trans_b = (((1,), (1,)), ((), ()))
    for gi in range(group):
        cols = slice(gi * HEAD_DIM, (gi + 1) * HEAD_DIM)
        q = q_ref[:, cols].astype(F32)
        if latent:
            q = _rope_half(q, tc_ref[rows, :], ts_ref[rows, :])
        q = (q * (HEAD_DIM ** -0.5 * LOG2E)).astype(BF16)
        sink = sink_ref[kvh * group + gi] * LOG2E
        s_c = lax.dot_general(q, kc, trans_b, preferred_element_type=F32)
        m = jnp.maximum(jnp.max(s_c, axis=1, keepdims=True), sink)
        if latent:
            s_w = lax.dot_general(q, k_win, trans_b, preferred_element_type=F32)
            s_w = jnp.where(in_window, s_w, NEG_INF)
            m = jnp.maximum(m, jnp.max(s_w, axis=1, keepdims=True))
        p_c = jnp.exp2(s_c - m)
        den = jnp.sum(p_c, axis=1, keepdims=True) + jnp.exp2(sink - m)
        o = jnp.dot(p_c.astype(BF16), vc, preferred_element_type=F32)
        if latent:
            p_w = jnp.exp2(s_w - m)
            den = den + jnp.sum(p_w, axis=1, keepdims=True)
            o = o + jnp.dot(p_w.astype(BF16), v_win, preferred_element_type=F32)
        o = o * (1.0 / den)
        y_ref[:, cols] = (o * _silu(ug_ref[:, cols].astype(F32))).astype(BF16)


def _mixer_b(u, uc, sink, tables, latent):
    tq = 256
    q_src = u if latent else uc
    rows = q_src.shape[0]
    tiles = rows // BATCH // tq
    width = (B_HEADS // B_KV_HEADS) * HEAD_DIM
    q0, k0, v0 = A_IN // width, (A_IN + 1024) // LANES, (A_IN + 1280) // LANES
    gate0 = (A_IN + B_IN + 1024) // width
    col = lambda off: (lambda b, h, qi: (b, off + h))
    in_specs = [
        pl.BlockSpec(memory_space=pltpu.SMEM),
        pl.BlockSpec((tq, width), lambda b, h, qi: (b * tiles + qi, q0 + h)),
        pl.BlockSpec((CTX_LEN, LANES), col(k0)),
        pl.BlockSpec((CTX_LEN, LANES), col(v0)),
        pl.BlockSpec((tq, width), lambda b, h, qi: (b * tiles + qi, gate0 + h)),
    ]
    args = [sink, q_src, uc, uc, q_src]
    scratch = []
    if latent:
        table = pl.BlockSpec((SEQ, LANES), lambda b, h, qi: (0, 0))
        in_specs += [pl.BlockSpec((SEQ, LANES), col(k0)), pl.BlockSpec((SEQ, LANES), col(v0)),
                     table, table]
        args += [u, u, tables["c128"], tables["s128"]]
        scratch = [pltpu.VMEM((SEQ, LANES), BF16)]
    return pl.pallas_call(
        functools.partial(_mixer_b_kernel, latent=latent, tq=tq),
        grid=(BATCH, B_KV_HEADS, tiles),
        in_specs=in_specs,
        out_specs=pl.BlockSpec((tq, width), lambda b, h, qi: (b * tiles + qi, h)),
        out_shape=jax.ShapeDtypeStruct((rows, B_HEADS * HEAD_DIM), BF16),
        scratch_shapes=scratch,
        compiler_params=_params(3),
        name="mixer_b" if latent else "mixer_b_ctx",
    )(*args)


def _mixer_c_kernel(*refs, latent, tq, tiles, n_group):
    if latent:
        (q_ref, kc_ref, vc_ref, ug_ref, qg_ref, kg_ref, kl_ref, vl_ref, tc_ref, ts_ref, qn_ref,
         y_ref, ks_ref, vs_ref, qs_ref, *att) = refs
    else:
        q_ref, kc_ref, vc_ref, ug_ref, qg_ref, kg_ref, y_ref, ks_ref, vs_ref, *att = refs
    qi = pl.program_id(2)
    group = C_HEADS // C_KV_HEADS

    @pl.when(qi == 0)
    def _():
        ks_ref[0:CTX_LEN, :] = _rms(kc_ref[...].astype(F32), kg_ref[...]).astype(BF16)
        _store_values(vs_ref, slice(0, CTX_LEN), vc_ref[...])
        if latent:
            def prep(rows, i):
                k = _rms(kl_ref[rows, :].astype(F32), kg_ref[...])
                k = _rope_half(k, tc_ref[rows, :], ts_ref[rows, :])
                dst = pl.ds(pl.multiple_of(CTX_LEN + i * PREP_ROWS, CTX_LEN), PREP_ROWS)
                ks_ref[dst, :] = k.astype(BF16)
                _store_values(vs_ref, dst, vl_ref[rows, :])
            _for_row_blocks(SEQ, PREP_ROWS, prep)

    def prep(src_ref, tile):
        if latent:
            rows = pl.ds(pl.multiple_of(tile * tq, tq), tq)
        qs = []
        for gi in range(group):
            q = _rms(src_ref[:, gi * HEAD_DIM:(gi + 1) * HEAD_DIM].astype(F32), qg_ref[...])
            if latent:
                q = _rope_half(q, tc_ref[rows, :], ts_ref[rows, :])
            qs.append((q * (HEAD_DIM ** -0.5 * LOG2E)).astype(BF16))
        return jnp.concatenate(qs, axis=0)

    attend = lambda qs: _attend(qs, ks_ref, vs_ref, att, n_group)
    if latent:
        o = _with_queries_ahead(prep, q_ref, qn_ref, qs_ref, C_KV_HEADS, tiles, attend)
    else:
        o = attend(prep(q_ref, 0))
    for gi in range(group):
        cols = slice(gi * HEAD_DIM, (gi + 1) * HEAD_DIM)
        y_ref[:, cols] = (o[gi * tq:(gi + 1) * tq] * _silu(ug_ref[:, cols].astype(F32))).astype(BF16)


def _mixer_c(u, uc, q_g, k_g, tables, latent):
    tq = CTX_LEN
    q_src = u if latent else uc
    rows = q_src.shape[0]
    tiles = rows // BATCH // tq
    group = C_HEADS // C_KV_HEADS
    width = group * HEAD_DIM
    n_group = N_KEY_GROUP if latent else 0
    n_keys = KEY_CHUNK + n_group * KEY_GROUP
    k0, v0, gate0 = 1024 // LANES, 1280 // LANES, 2048 // width
    col = lambda off: (lambda b, h, qi: (b, off + h))
    in_specs = [
        pl.BlockSpec((tq, width), lambda b, h, qi: (b * tiles + qi, h)),
        pl.BlockSpec((CTX_LEN, LANES), col(k0)),
        pl.BlockSpec((CTX_LEN, LANES), col(v0)),
        pl.BlockSpec((tq, width), lambda b, h, qi: (b * tiles + qi, gate0 + h)),
        pl.BlockSpec((1, LANES), lambda b, h, qi: (0, 0)),
        pl.BlockSpec((1, LANES), lambda b, h, qi: (0, 0)),
    ]
    args = [q_src, uc, uc, q_src, q_g, k_g]
    if latent:
        table = pl.BlockSpec((SEQ, LANES), lambda b, h, qi: (0, 0))

        def next_q(b, h, qi):
            b, h, qi = _next_step(b, h, qi, C_KV_HEADS, tiles)
            return b * tiles + qi, h
        in_specs += [pl.BlockSpec((SEQ, LANES), col(k0)), pl.BlockSpec((SEQ, LANES), col(v0)),
                     table, table, pl.BlockSpec((tq, width), next_q)]
        args += [u, u, tables["c128"], tables["s128"], u]
    return pl.pallas_call(
        functools.partial(_mixer_c_kernel, latent=latent, tq=tq, tiles=tiles, n_group=n_group),
        grid=(BATCH, C_KV_HEADS, tiles),
        in_specs=in_specs,
        out_specs=pl.BlockSpec((tq, width), lambda b, h, qi: (b * tiles + qi, h)),
        out_shape=jax.ShapeDtypeStruct((rows, C_HEADS * HEAD_DIM), BF16),
        scratch_shapes=[pltpu.VMEM((n_keys, LANES), BF16), pltpu.VMEM((n_keys, 2 * LANES), BF16)]
        + ([pltpu.VMEM((2, group * tq, LANES), BF16)] if latent else [])
        + _attend_scratch(group * tq, n_group),
        compiler_params=_params(3),
        name="mixer_c" if latent else "mixer_c_ctx",
    )(*args)


def _mixer_d_kernel(*refs, latent, tq, n_group):
    if latent:
        (q_ref, knc_ref, vc_ref, pec_ref, ug_ref, knl_ref, vl_ref, pel_ref, tc_ref, tsa_ref,
         tsb_ref, y_ref, ks_ref, vs_ref, *att) = refs
    else:
        q_ref, knc_ref, vc_ref, pec_ref, ug_ref, y_ref, ks_ref, vs_ref, *att = refs
    qi = pl.program_id(2)

    @pl.when(qi == 0)
    def _():
        ks_ref[0:CTX_LEN, 0:D_NOPE] = knc_ref[...]
        ks_ref[0:CTX_LEN, D_NOPE:2 * D_NOPE] = pec_ref[...]
        _store_values(vs_ref, slice(0, CTX_LEN), vc_ref[...])
        if latent:
            def prep(rows, i):
                pe = _rope_quarter(pel_ref[rows, :].astype(F32), tc_ref[rows, :], tsa_ref[rows, :],
                                   tsb_ref[rows, :])
                dst = pl.ds(pl.multiple_of(CTX_LEN + i * PREP_ROWS, CTX_LEN), PREP_ROWS)
                ks_ref[dst, 0:D_NOPE] = knl_ref[rows, :]
                ks_ref[dst, D_NOPE:2 * D_NOPE] = pe.astype(BF16)
                _store_values(vs_ref, dst, vl_ref[rows, :])
            _for_row_blocks(SEQ, PREP_ROWS, prep)

    def prep(src_ref, tile):
        scale = (D_NOPE + D_ROPE) ** -0.5 * LOG2E
        q_nope = src_ref[:, 0:D_NOPE].astype(F32)
        q_pe = src_ref[:, D_NOPE:2 * D_NOPE].astype(F32)
        if latent:
            rows = pl.ds(pl.multiple_of(tile * tq, tq), tq)
            q_pe = _rope_quarter(q_pe, tc_ref[rows, :], tsa_ref[rows, :], tsb_ref[rows, :])
        return jnp.concatenate([(q_nope * scale).astype(BF16), (q_pe * scale).astype(BF16)], axis=1)

    attend = lambda q: _attend(q, ks_ref, vs_ref, att, n_group)
    o = attend(prep(q_ref, qi))
    y_ref[...] = (o * _silu(ug_ref[...].astype(F32))).astype(BF16)


def _mixer_d(u, uc, qd, kvd, qdc, kvdc, tables, latent):
    tq = 1024 if latent else CTX_LEN
    q_src, ug_src = (qd, u) if latent else (qdc, uc)
    rows = q_src.shape[0]
    tiles = rows // BATCH // tq
    n_group = N_KEY_GROUP if latent else 0
    n_keys = KEY_CHUNK + n_group * KEY_GROUP
    pe0, gate0 = 4352 // LANES, (2048 + 1024) // LANES
    in_specs = [
        pl.BlockSpec((tq, 2 * LANES), lambda b, h, qi: (b * tiles + qi, h)),
        pl.BlockSpec((CTX_LEN, LANES), lambda b, h, qi: (b, 2 * h)),
        pl.BlockSpec((CTX_LEN, LANES), lambda b, h, qi: (b, 2 * h + 1)),
        pl.BlockSpec((CTX_LEN, LANES), lambda b, h, qi: (b, pe0)),
        pl.BlockSpec((tq, LANES), lambda b, h, qi: (b * tiles + qi, gate0 + h)),
    ]
    args = [q_src, kvdc, kvdc, uc, ug_src]
    if latent:
        table = pl.BlockSpec((SEQ, LANES), lambda b, h, qi: (0, 0))
        in_specs += [
            pl.BlockSpec((SEQ, LANES), lambda b, h, qi: (b, 2 * h)),
            pl.BlockSpec((SEQ, LANES), lambda b, h, qi: (b, 2 * h + 1)),
            pl.BlockSpec((SEQ, LANES), lambda b, h, qi: (b, pe0)),
            table, table, table]
        args += [kvd, kvd, u, tables["cd"], tables["sda"], tables["sdb"]]
    return pl.pallas_call(
        functools.partial(_mixer_d_kernel, latent=latent, tq=tq, n_group=n_group),
        grid=(BATCH, D_HEADS, tiles),
        in_specs=in_specs,
        out_specs=pl.BlockSpec((tq, LANES), lambda b, h, qi: (b * tiles + qi, h)),
        out_shape=jax.ShapeDtypeStruct((rows, D_HEADS * HEAD_DIM), BF16),
        scratch_shapes=[pltpu.VMEM((n_keys, 2 * LANES), BF16)] * 2
        + _attend_scratch(tq, n_group),
        compiler_params=_params(3),
        name="mixer_d" if latent else "mixer_d_ctx",
    )(*args)


def _rope_tables():
    n = jnp.arange(SEQ)
    row = (n // GRID_W).astype(F32)
    col = (n % GRID_W).astype(F32)

    def cos_sin(dim):
        n_freq = dim // 4
        inv = ROPE_BASE ** (-jnp.arange(n_freq, dtype=F32) / n_freq)
        ang = jnp.concatenate([row[:, None] * inv[None], col[:, None] * inv[None]], axis=-1)
        return jnp.cos(ang), jnp.sin(ang)

    c, s = cos_sin(HEAD_DIM)
    tables = {"c128": jnp.concatenate([c, c], 1), "s128": jnp.concatenate([-s, s], 1)}
    c, s = cos_sin(A_QK_DIM)
    z = jnp.zeros_like(c)
    tables["ca"] = jnp.concatenate([c, c, c, c], 1)
    tables["saa"] = jnp.concatenate([-s, z, -s, z], 1)
    tables["sab"] = jnp.concatenate([z, s, z, s], 1)
    tables["cd"] = jnp.concatenate([c, c, z, z], 1)
    tables["sda"] = jnp.concatenate([-s, z, z, z], 1)
    tables["sdb"] = jnp.concatenate([z, s, z, z], 1)
    return tables


def _odd_weights(w):
    w = w.astype(BF16)
    d0 = C_IN
    gate0 = C_IN + D_IN
    pieces = [w[..., :d0 + D_Q_RANK], w[..., gate0:], w[..., d0 + D_Q_RANK:gate0]]
    used = sum(p.shape[-1] for p in pieces)
    pieces.append(jnp.zeros(w.shape[:-1] + (ODD_PAD - used,), BF16))
    return jnp.concatenate(pieces, axis=-1)


def kernel(x, c, ctx, c_ctx, w_mod, b_mod, norm_g, w_o, final_g, e_w_in, a_lam_q1, a_lam_k1, a_lam_q2, a_lam_k2, a_sub_g, b_sink, o_w_in, c_q_g, c_k_g, d_q_a_g, d_kv_a_g, d_w_q_b, d_w_kv_b):
    xl = x.reshape(N_LAT_ROWS, D_MODEL)
    xc = ctx.reshape(N_CTX_ROWS, D_MODEL)
    cc = jnp.concatenate([c, c_ctx[None], jnp.zeros((8 - BATCH - 1, D_MODEL), F32)], axis=0)
    mod = _modulation(cc, w_mod, b_mod)
    tables = _rope_tables()
    w_out = w_o.astype(BF16)
    w_even = e_w_in.astype(BF16)
    w_odd = _odd_weights(o_w_in)

    for l in range(DEPTH):
        with_ctx = l < DEPTH - 1
        i = l // 2
        shift, scale, gate = (mod[l, :BATCH, k * D_MODEL:(k + 1) * D_MODEL][:, None, :]
                              for k in range(3))
        cshift, cscale, cgate = (mod[l, BATCH:BATCH + 1, k * D_MODEL:(k + 1) * D_MODEL][:, None, :]
                                 for k in range(3))
        g = norm_g[l][None]
        if l % 2 == 0:
            u = _in_proj(xl, scale, shift, g, w_even, i, "in_proj_even")
            uc = _in_proj(xc, cscale, cshift, g, w_even, i, "in_proj_even_ctx")
            lam_init = 0.8 - 0.6 * math.exp(-0.3 * l)
            lam_vecs = jnp.stack([a_lam_q1[i], a_lam_k1[i], a_lam_q2[i], a_lam_k2[i]]).astype(F32)
            sub_g = a_sub_g[i][None]
            y1 = _mixer_a(u, uc, lam_vecs, sub_g, tables, lam_init, True)
            y2 = _mixer_b(u, uc, b_sink[i], tables, True)
            if with_ctx:
                yc1 = _mixer_a(u, uc, lam_vecs, sub_g, tables, lam_init, False)
                yc2 = _mixer_b(u, uc, b_sink[i], tables, False)
        else:
            u = _in_proj(xl, scale, shift, g, w_odd, i, "in_proj_odd")
            uc = _in_proj(xc, cscale, cshift, g, w_odd, i, "in_proj_odd_ctx")
            wq = jnp.pad(d_w_q_b[i].reshape(D_Q_RANK, D_HEADS, D_NOPE + D_ROPE),
                         ((0, 0), (0, 0), (0, 2 * LANES - D_NOPE - D_ROPE)))
            wq = wq.reshape(D_Q_RANK, D_HEADS * 2 * LANES).astype(BF16)
            wkv = d_w_kv_b[i].astype(BF16)
            qd, kvd = _dproj(u, d_q_a_g[i][None], d_kv_a_g[i][None], wq, wkv, "dproj")
            qdc, kvdc = _dproj(uc, d_q_a_g[i][None], d_kv_a_g[i][None], wq, wkv, "dproj_ctx")
            y1 = _mixer_c(u, uc, c_q_g[i][None], c_k_g[i][None], tables, True)
            y2 = _mixer_d(u, uc, qd, kvd, qdc, kvdc, tables, True)
            if with_ctx:
                yc1 = _mixer_c(u, uc, c_q_g[i][None], c_k_g[i][None], tables, False)
                yc2 = _mixer_d(u, uc, qd, kvd, qdc, kvdc, tables, False)
        last = l == DEPTH - 1
        xl = _out_proj(xl, y1, y2, w_out, l, gate, final_g[None] if last else None,
                       "out_proj_final" if last else "out_proj")
        if with_ctx:
            xc = _out_proj(xc, yc1, yc2, w_out, l, cgate, None, "out_proj_ctx")
    return xl.reshape(BATCH, SEQ, D_MODEL)
```

```python
import functools
import math

import jax
import jax.numpy as jnp
from jax import lax
from jax.experimental import pallas as pl
from jax.experimental.pallas import tpu as pltpu

F32 = jnp.float32
BF16 = jnp.bfloat16

D_MODEL = 2048
BATCH = 2
SEQ = 4096
DEPTH = 4
GRID_W = 64
CTX_LEN = 256
HEAD_DIM = 128
ROPE_BASE = 10000.0
EPS = 1e-6
NEG_INF = -1e30
WINDOW = 128

A_HEADS = 8
A_QK_DIM = 64
B_HEADS = 8
B_KV_HEADS = 2
C_HEADS = 8
C_KV_HEADS = 2
D_HEADS = 8
D_Q_RANK = 512
D_KV_RANK = 256
D_NOPE = 128
D_ROPE = 64

A_IN = 3072
B_IN = 1536
C_IN = 1536
D_IN = 832
MIX_WIDTH = 2048
EVEN_IN = A_IN + B_IN + MIX_WIDTH
ODD_IN = C_IN + D_IN + MIX_WIDTH
ODD_PAD = 4608

LOG2E = math.log2(math.e)
LANES = 128
KEY_CHUNK = 256
KEY_GROUP = 1024
N_KEY_GROUP = SEQ // KEY_GROUP
N_LAT_ROWS = BATCH * SEQ
N_CTX_ROWS = BATCH * CTX_LEN
VMEM_LIMIT = 48 * 1024 * 1024


def _params(n_grid_dims, vmem=VMEM_LIMIT):
    return pltpu.CompilerParams(
        dimension_semantics=("arbitrary",) * n_grid_dims, vmem_limit_bytes=vmem)


def _silu(v):
    return v * (1.0 / (1.0 + jnp.exp(-v)))


def _rms(v, g):
    return v * lax.rsqrt(jnp.mean(v * v, axis=-1, keepdims=True) + EPS) * g


def _mod_kernel(cc_ref, w_ref, b_ref, o_ref):
    s = _silu(cc_ref[...]).astype(BF16)
    o_ref[0] = jnp.dot(s, w_ref[0].astype(BF16), preferred_element_type=F32) + b_ref[0]


def _modulation(cc, w_mod, b_mod):
    tn = 1024
    return pl.pallas_call(
        _mod_kernel,
        grid=(DEPTH, 3 * D_MODEL // tn),
        in_specs=[
            pl.BlockSpec((8, D_MODEL), lambda l, j: (0, 0)),
            pl.BlockSpec((1, D_MODEL, tn), lambda l, j: (l, 0, j)),
            pl.BlockSpec((1, 1, tn), lambda l, j: (l, 0, j)),
        ],
        out_specs=pl.BlockSpec((1, 8, tn), lambda l, j: (l, 0, j)),
        out_shape=jax.ShapeDtypeStruct((DEPTH, 8, 3 * D_MODEL), F32),
        compiler_params=_params(2),
        name="modulation",
    )(cc, w_mod, b_mod.reshape(DEPTH, 1, 3 * D_MODEL))


def _in_proj_kernel(x0_ref, sc0_ref, sh0_ref, xn_ref, scn_ref, shn_ref, g_ref, w_ref, o_ref,
                    ha_ref, hb_ref, *, n_col_tiles):
    i = pl.program_id(0)
    j = pl.program_id(1)

    def normed(x_ref, sc_ref, sh_ref):
        return (_rms(x_ref[...], g_ref[...]) * (1.0 + sc_ref[0]) + sh_ref[0]).astype(BF16)

    @pl.when((i == 0) & (j == 0))
    def _():
        ha_ref[...] = normed(x0_ref, sc0_ref, sh0_ref)

    for parity, (cur_ref, nxt_ref) in enumerate(((ha_ref, hb_ref), (hb_ref, ha_ref))):
        @pl.when((i % 2 == parity) & (j < n_col_tiles - 1))
        def _():
            o_ref[...] = jnp.dot(cur_ref[...], w_ref[...], preferred_element_type=F32).astype(BF16)

        @pl.when((i % 2 == parity) & (j == n_col_tiles - 1))
        def _():
            nxt_ref[...] = normed(xn_ref, scn_ref, shn_ref)
            o_ref[...] = jnp.dot(cur_ref[...], w_ref[...], preferred_element_type=F32).astype(BF16)


def _in_proj(x, scale, shift, g, w, layer, name):
    rows, n_out = x.shape[0], w.shape[2]
    tm = 512
    tn = n_out // (4 if n_out == EVEN_IN else 3)
    n_row_tiles = rows // tm
    tiles_per_mod = n_row_tiles // scale.shape[0]
    nxt = lambda i: jnp.minimum(i + 1, n_row_tiles - 1)
    mod_spec = lambda row_tile: pl.BlockSpec(
        (1, 1, D_MODEL), lambda i, j: (row_tile(i) // tiles_per_mod, 0, 0))
    first = lambda i: 0
    return pl.pallas_call(
        functools.partial(_in_proj_kernel, n_col_tiles=n_out // tn),
        grid=(n_row_tiles, n_out // tn),
        in_specs=[
            pl.BlockSpec((tm, D_MODEL), lambda i, j: (0, 0)),
            mod_spec(first),
            mod_spec(first),
            pl.BlockSpec((tm, D_MODEL), lambda i, j: (nxt(i), 0)),
            mod_spec(nxt),
            mod_spec(nxt),
            pl.BlockSpec((1, D_MODEL), lambda i, j: (0, 0)),
            pl.BlockSpec((None, D_MODEL, tn), lambda i, j: (layer, 0, j)),
        ],
        out_specs=pl.BlockSpec((tm, tn), lambda i, j: (i, j)),
        out_shape=jax.ShapeDtypeStruct((rows, n_out), BF16),
        scratch_shapes=[pltpu.VMEM((tm, D_MODEL), BF16)] * 2,
        compiler_params=_params(2),
        name=name,
    )(x, scale, shift, x, scale, shift, g, w)


def _out_proj_kernel(*refs, final):
    if final:
        x_ref, y1_ref, y2_ref, w_ref, gate_ref, fg_ref, o_ref = refs
    else:
        x_ref, y1_ref, y2_ref, w_ref, gate_ref, o_ref = refs
    half = MIX_WIDTH // 2
    d = jnp.dot(y1_ref[...], w_ref[0:half, :], preferred_element_type=F32)
    d = d + jnp.dot(y2_ref[...], w_ref[half:MIX_WIDTH, :], preferred_element_type=F32)
    xn = x_ref[...] + gate_ref[0] * d
    if final:
        xn = _rms(xn, fg_ref[...])
    o_ref[...] = xn


def _out_proj(x, y1, y2, w, layer, gate, final_g, name):
    rows = x.shape[0]
    tm = 512
    tiles_per_mod = rows // tm // gate.shape[0]
    half = MIX_WIDTH // 2
    in_specs = [
        pl.BlockSpec((tm, D_MODEL), lambda i: (i, 0)),
        pl.BlockSpec((tm, half), lambda i: (i, 0)),
        pl.BlockSpec((tm, half), lambda i: (i, 0)),
        pl.BlockSpec((None, MIX_WIDTH, D_MODEL), lambda i: (layer, 0, 0)),
        pl.BlockSpec((1, 1, D_MODEL), lambda i: (i // tiles_per_mod, 0, 0)),
    ]
    args = [x, y1, y2, w, gate]
    if final_g is not None:
        in_specs.append(pl.BlockSpec((1, D_MODEL), lambda i: (0, 0)))
        args.append(final_g)
    return pl.pallas_call(
        functools.partial(_out_proj_kernel, final=final_g is not None),
        grid=(rows // tm,),
        in_specs=in_specs,
        out_specs=pl.BlockSpec((tm, D_MODEL), lambda i: (i, 0)),
        out_shape=jax.ShapeDtypeStruct((rows, D_MODEL), F32),
        compiler_params=_params(1, 56 * 1024 * 1024),
        name=name,
    )(*args)


def _dproj_kernel(cq_ref, ckv_ref, qg_ref, kvg_ref, wq_ref, wkv_ref, q_out, kv_out):
    cq = _rms(cq_ref[...].astype(F32), qg_ref[...]).astype(BF16)
    ckv = _rms(ckv_ref[...].astype(F32), kvg_ref[...]).astype(BF16)
    q_out[...] = jnp.dot(cq, wq_ref[...], preferred_element_type=F32).astype(BF16)
    kv_out[...] = jnp.dot(ckv, wkv_ref[...], preferred_element_type=F32).astype(BF16)


def _dproj(u, q_g, kv_g, wq, wkv, name):
    rows = u.shape[0]
    tm = 512
    n_q, n_kv = wq.shape[1], wkv.shape[1]
    return pl.pallas_call(
        _dproj_kernel,
        grid=(rows // tm,),
        in_specs=[
            pl.BlockSpec((tm, D_Q_RANK), lambda i: (i, 1536 // D_Q_RANK)),
            pl.BlockSpec((tm, D_KV_RANK), lambda i: (i, 4096 // D_KV_RANK)),
            pl.BlockSpec((1, D_Q_RANK), lambda i: (0, 0)),
            pl.BlockSpec((1, D_KV_RANK), lambda i: (0, 0)),
            pl.BlockSpec((D_Q_RANK, n_q), lambda i: (0, 0)),
            pl.BlockSpec((D_KV_RANK, n_kv), lambda i: (0, 0)),
        ],
        out_specs=[
            pl.BlockSpec((tm, n_q), lambda i: (i, 0)),
            pl.BlockSpec((tm, n_kv), lambda i: (i, 0)),
        ],
        out_shape=[
            jax.ShapeDtypeStruct((rows, n_q), BF16),
            jax.ShapeDtypeStruct((rows, n_kv), BF16),
        ],
        compiler_params=_params(1),
        name=name,
    )(u, u, q_g, kv_g, wq, wkv)


def _rope_half(x, c, s):
    return x * c + pltpu.roll(x, 64, 1) * s


def _rope_quarter(x, c, sa, sb):
    return x * c + pltpu.roll(x, 96, 1) * sa + pltpu.roll(x, 32, 1) * sb


def _attend(q, k_ref, v_ref, att, n_group):
    sc_ref, mb_ref = att[0], att[1]
    sl_ref = att[2] if n_group else None
    half = q.shape[0] // 2
    parts = [slice(0, half), slice(half, 2 * half)]
    trans_b = (((1,), (1,)), ((), ()))

    def lane_fold_max(s):
        parts = [s[:, j * LANES:(j + 1) * LANES] for j in range(s.shape[1] // LANES)]
        while len(parts) > 1:
            odd = [parts[-1]] if len(parts) % 2 else []
            parts = [jnp.maximum(a, b) for a, b in zip(parts[0::2], parts[1::2])] + odd
        return parts[0]

    def group_rows(g):
        return slice(KEY_CHUNK + g * KEY_GROUP, KEY_CHUNK + (g + 1) * KEY_GROUP)

    ms = []
    for r in parts:
        s = lax.dot_general(q[r], k_ref[0:KEY_CHUNK, :], trans_b, preferred_element_type=F32)
        sc_ref[r, :] = s
        ms.append(lane_fold_max(s))
    for g in range(n_group):
        for i, r in enumerate(parts):
            s = lax.dot_general(q[r], k_ref[group_rows(g), :], trans_b,
                                preferred_element_type=F32)
            sl_ref[g, r, :] = s
            ms[i] = jnp.maximum(ms[i], lane_fold_max(s))
    for m, r in zip(ms, parts):
        mb_ref[r, :] = jnp.broadcast_to(jnp.max(m, axis=1, keepdims=True), (half, LANES))

    def weights(s, mb):
        cols = [jnp.exp2(s[:, j * LANES:(j + 1) * LANES] - mb) for j in range(s.shape[1] // LANES)]
        return jnp.concatenate(cols, axis=1).astype(BF16)

    accs = [jnp.dot(weights(sc_ref[r, :], mb_ref[r, :]), v_ref[0:KEY_CHUNK, :],
                    preferred_element_type=F32) for r in parts]
    for g in range(n_group):
        for i, r in enumerate(parts):
            accs[i] = accs[i] + jnp.dot(weights(sl_ref[g, r, :], mb_ref[r, :]),
                                        v_ref[group_rows(g), :], preferred_element_type=F32)
    return jnp.concatenate([a[:, :LANES] * (1.0 / a[:, LANES:]) for a in accs], axis=0)


def _attend_scratch(m_rows, n_group):
    scratch = [pltpu.VMEM((m_rows, KEY_CHUNK), F32), pltpu.VMEM((m_rows, LANES), F32)]
    if n_group:
        scratch.append(pltpu.VMEM((n_group, m_rows, KEY_GROUP), F32))
    return scratch


def _store_values(vs_ref, rows, v):
    vs_ref[rows, 0:LANES] = v
    vs_ref[rows, LANES:2 * LANES] = jnp.ones(v.shape, BF16)


def _for_row_blocks(n_rows, block, fn):
    def body(i, carry):
        fn(pl.ds(pl.multiple_of(i * block, block), block), i)
        return carry
    lax.fori_loop(0, n_rows // block, body, 0)


PREP_ROWS = 512


def _next_step(b, h, qi, n_heads, tiles):
    t = jnp.minimum((b * n_heads + h) * tiles + qi + 1, BATCH * n_heads * tiles - 1)
    return t // (n_heads * tiles), (t // tiles) % n_heads, t % tiles


def _with_queries_ahead(prep, q_ref, qn_ref, qs_ref, n_heads, tiles, attend):
    b, h, qi = pl.program_id(0), pl.program_id(1), pl.program_id(2)
    step = (b * n_heads + h) * tiles + qi
    slot = step % 2

    @pl.when(step == 0)
    def _():
        qs_ref[0] = prep(q_ref, qi)

    out = attend(qs_ref[slot])
    qs_ref[1 - slot] = prep(qn_ref, _next_step(b, h, qi, n_heads, tiles)[2])
    return out


def _mixer_a_kernel(*refs, latent, lam_init, tq, n_group):
    if latent:
        (q_ref, kc_ref, vc_ref, ug_ref, lam_ref, subg_ref, kl_ref, vl_ref, tc_ref, tsa_ref,
         tsb_ref, y_ref, ks_ref, vs_ref, *att) = refs
    else:
        q_ref, kc_ref, vc_ref, ug_ref, lam_ref, subg_ref, y_ref, ks_ref, vs_ref, *att = refs
    qi = pl.program_id(2)

    @pl.when(qi == 0)
    def _():
        ks_ref[0:CTX_LEN, :] = kc_ref[...]
        _store_values(vs_ref, slice(0, CTX_LEN), vc_ref[...])
        if latent:
            def prep(rows, i):
                k = _rope_quarter(kl_ref[rows, :].astype(F32), tc_ref[rows, :], tsa_ref[rows, :],
                                  tsb_ref[rows, :])
                dst = pl.ds(pl.multiple_of(CTX_LEN + i * PREP_ROWS, CTX_LEN), PREP_ROWS)
                ks_ref[dst, :] = k.astype(BF16)
                _store_values(vs_ref, dst, vl_ref[rows, :])
            _for_row_blocks(SEQ, PREP_ROWS, prep)

    def prep(src_ref, tile):
        q = src_ref[...].astype(F32)
        if latent:
            rows = pl.ds(pl.multiple_of(tile * tq, tq), tq)
            q = _rope_quarter(q, tc_ref[rows, :], tsa_ref[rows, :], tsb_ref[rows, :])
        q = q * (A_QK_DIM ** -0.5 * LOG2E)
        lane = lax.broadcasted_iota(jnp.int32, (1, LANES), 1)
        lo = (lane < A_QK_DIM).astype(F32)
        return jnp.concatenate([(q * lo).astype(BF16), (q * (1.0 - lo)).astype(BF16)], axis=0)

    attend = lambda qs: _attend(qs, ks_ref, vs_ref, att, n_group)
    o = attend(prep(q_ref, qi))
    lv = lam_ref[...]
    lam = (jnp.exp(jnp.sum(lv[0:1] * lv[1:2], axis=1, keepdims=True))
           - jnp.exp(jnp.sum(lv[2:3] * lv[3:4], axis=1, keepdims=True)) + lam_init)
    o = o[0:tq] - lam * o[tq:2 * tq]
    y = _rms(o, subg_ref[...]) * (1.0 - lam_init)
    y_ref[...] = (y * _silu(ug_ref[...].astype(F32))).astype(BF16)


def _mixer_a(u, uc, lam_vecs, sub_g, tables, lam_init, latent):
    tq = 512 if latent else CTX_LEN
    q_src = u if latent else uc
    rows = q_src.shape[0]
    tiles = rows // BATCH // tq
    n_group = N_KEY_GROUP if latent else 0
    n_keys = KEY_CHUNK + n_group * KEY_GROUP
    gate0 = (A_IN + B_IN) // LANES
    col = lambda off: (lambda b, h, qi: (b, off + h))
    in_specs = [
        pl.BlockSpec((tq, LANES), lambda b, h, qi: (b * tiles + qi, h)),
        pl.BlockSpec((CTX_LEN, LANES), col(8)),
        pl.BlockSpec((CTX_LEN, LANES), col(16)),
        pl.BlockSpec((tq, LANES), lambda b, h, qi: (b * tiles + qi, gate0 + h)),
        pl.BlockSpec((4, A_QK_DIM), lambda b, h, qi: (0, 0)),
        pl.BlockSpec((1, LANES), lambda b, h, qi: (0, 0)),
    ]
    args = [q_src, uc, uc, q_src, lam_vecs, sub_g]
    if latent:
        table = pl.BlockSpec((SEQ, LANES), lambda b, h, qi: (0, 0))
        in_specs += [pl.BlockSpec((SEQ, LANES), col(8)), pl.BlockSpec((SEQ, LANES), col(16)),
                     table, table, table]
        args += [u, u, tables["ca"], tables["saa"], tables["sab"]]
    return pl.pallas_call(
        functools.partial(_mixer_a_kernel, latent=latent, lam_init=lam_init, tq=tq,
                          n_group=n_group),
        grid=(BATCH, A_HEADS, tiles),
        in_specs=in_specs,
        out_specs=pl.BlockSpec((tq, LANES), lambda b, h, qi: (b * tiles + qi, h)),
        out_shape=jax.ShapeDtypeStruct((rows, A_HEADS * HEAD_DIM), BF16),
        scratch_shapes=[pltpu.VMEM((n_keys, LANES), BF16), pltpu.VMEM((n_keys, 2 * LANES), BF16)]
        + _attend_scratch(2 * tq, n_group),
        compiler_params=_params(3),
        name="mixer_a" if latent else "mixer_a_ctx",
    )(*args)


def _mixer_b_kernel(*refs, latent, tq):
    if latent:
        (sink_ref, q_ref, kc_ref, vc_ref, ug_ref, kl_ref, vl_ref, tc_ref, ts_ref, y_ref,
         ks_ref) = refs
    else:
        sink_ref, q_ref, kc_ref, vc_ref, ug_ref, y_ref = refs
    kvh = pl.program_id(1)
    qi = pl.program_id(2)
    group = B_HEADS // B_KV_HEADS
    span = tq + 2 * WINDOW

    if latent:
        @pl.when(qi == 0)
        def _():
            def prep(rows, i):
                ks_ref[rows, :] = _rope_half(kl_ref[rows, :].astype(F32), tc_ref[rows, :],
                                             ts_ref[rows, :]).astype(BF16)
            _for_row_blocks(SEQ, PREP_ROWS, prep)

        rows = pl.ds(pl.multiple_of(qi * tq, tq), tq)
        start = pl.multiple_of(jnp.clip(qi * tq - WINDOW, 0, SEQ - span), WINDOW)
        k_win = ks_ref[pl.ds(start, span), :]
        v_win = vl_ref[pl.ds(start, span), :]
        q_pos = qi * tq + lax.broadcasted_iota(jnp.int32, (tq, span), 0)
        k_pos = start + lax.broadcasted_iota(jnp.int32, (tq, span), 1)
        in_window = jnp.abs(k_pos - q_pos) <= WINDOW

    def lane_fold_max(s):
        m = s[:, 0:LANES]
        for j in range(1, s.shape[1] // LANES):
            m = jnp.maximum(m, s[:, j * LANES:(j + 1) * LANES])
        return m

    kc = kc_ref[...]
    vc = jnp.concatenate([vc_ref[...], jnp.ones((CTX_LEN, LANES), BF16)], axis=1)
    if latent:
        v_win = jnp.concatenate([v_win, jnp.ones((span, LANES), BF16)], axis=1)
    trans_b = (((1,), (1,)), ((), ()))
    for gi in range(group):
        cols = slice(gi * HEAD_DIM, (gi + 1) * HEAD_DIM)
        q = q_ref[:, cols].astype(F32)
        if latent:
            q = _rope_half(q, tc_ref[rows, :], ts_ref[rows, :])
        q = (q * (HEAD_DIM ** -0.5 * LOG2E)).astype(BF16)
        sink = sink_ref[kvh * group + gi] * LOG2E
        s_c = lax.dot_general(q, kc, trans_b, preferred_element_type=F32)
        mf = lane_fold_max(s_c)
        if latent:
            s_w = lax.dot_general(q, k_win, trans_b, preferred_element_type=F32)
            s_w = jnp.where(in_window, s_w, NEG_INF)
            mf = jnp.maximum(mf, lane_fold_max(s_w))
        m = jnp.maximum(jnp.max(mf, axis=1, keepdims=True), sink)
        acc = jnp.dot(jnp.exp2(s_c - m).astype(BF16), vc, preferred_element_type=F32)
        if latent:
            acc = acc + jnp.dot(jnp.exp2(s_w - m).astype(BF16), v_win,
                                preferred_element_type=F32)
        den = acc[:, LANES:] + jnp.exp2(sink - m)
        o = acc[:, :LANES] * (1.0 / den)
        y_ref[:, cols] = (o * _silu(ug_ref[:, cols].astype(F32))).astype(BF16)


def _mixer_b(u, uc, sink, tables, latent):
    tq = 256
    q_src = u if latent else uc
    rows = q_src.shape[0]
    tiles = rows // BATCH // tq
    width = (B_HEADS // B_KV_HEADS) * HEAD_DIM
    q0, k0, v0 = A_IN // width, (A_IN + 1024) // LANES, (A_IN + 1280) // LANES
    gate0 = (A_IN + B_IN + 1024) // width
    col = lambda off: (lambda b, h, qi: (b, off + h))
    in_specs = [
        pl.BlockSpec(memory_space=pltpu.SMEM),
        pl.BlockSpec((tq, width), lambda b, h, qi: (b * tiles + qi, q0 + h)),
        pl.BlockSpec((CTX_LEN, LANES), col(k0)),
        pl.BlockSpec((CTX_LEN, LANES), col(v0)),
        pl.BlockSpec((tq, width), lambda b, h, qi: (b * tiles + qi, gate0 + h)),
    ]
    args = [sink, q_src, uc, uc, q_src]
    scratch = []
    if latent:
        table = pl.BlockSpec((SEQ, LANES), lambda b, h, qi: (0, 0))
        in_specs += [pl.BlockSpec((SEQ, LANES), col(k0)), pl.BlockSpec((SEQ, LANES), col(v0)),
                     table, table]
        args += [u, u, tables["c128"], tables["s128"]]
        scratch = [pltpu.VMEM((SEQ, LANES), BF16)]
    return pl.pallas_call(
        functools.partial(_mixer_b_kernel, latent=latent, tq=tq),
        grid=(BATCH, B_KV_HEADS, tiles),
        in_specs=in_specs,
        out_specs=pl.BlockSpec((tq, width), lambda b, h, qi: (b * tiles + qi, h)),
        out_shape=jax.ShapeDtypeStruct((rows, B_HEADS * HEAD_DIM), BF16),
        scratch_shapes=scratch,
        compiler_params=_params(3),
        name="mixer_b" if latent else "mixer_b_ctx",
    )(*args)


def _mixer_c_kernel(*refs, latent, tq, tiles, n_group):
    if latent:
        (q_ref, kc_ref, vc_ref, ug_ref, qg_ref, kg_ref, kl_ref, vl_ref, tc_ref, ts_ref, qn_ref,
         y_ref, ks_ref, vs_ref, qs_ref, *att) = refs
    else:
        q_ref, kc_ref, vc_ref, ug_ref, qg_ref, kg_ref, y_ref, ks_ref, vs_ref, *att = refs
    qi = pl.program_id(2)
    group = C_HEADS // C_KV_HEADS

    @pl.when(qi == 0)
    def _():
        ks_ref[0:CTX_LEN, :] = _rms(kc_ref[...].astype(F32), kg_ref[...]).astype(BF16)
        _store_values(vs_ref, slice(0, CTX_LEN), vc_ref[...])
        if latent:
            def prep(rows, i):
                k = _rms(kl_ref[rows, :].astype(F32), kg_ref[...])
                k = _rope_half(k, tc_ref[rows, :], ts_ref[rows, :])
                dst = pl.ds(pl.multiple_of(CTX_LEN + i * PREP_ROWS, CTX_LEN), PREP_ROWS)
                ks_ref[dst, :] = k.astype(BF16)
                _store_values(vs_ref, dst, vl_ref[rows, :])
            _for_row_blocks(SEQ, PREP_ROWS, prep)

    def prep(src_ref, tile):
        if latent:
            rows = pl.ds(pl.multiple_of(tile * tq, tq), tq)
        qs = []
        for gi in range(group):
            q = _rms(src_ref[:, gi * HEAD_DIM:(gi + 1) * HEAD_DIM].astype(F32), qg_ref[...])
            if latent:
                q = _rope_half(q, tc_ref[rows, :], ts_ref[rows, :])
            qs.append((q * (HEAD_DIM ** -0.5 * LOG2E)).astype(BF16))
        return jnp.concatenate(qs, axis=0)

    attend = lambda qs: _attend(qs, ks_ref, vs_ref, att, n_group)
    if latent:
        o = _with_queries_ahead(prep, q_ref, qn_ref, qs_ref, C_KV_HEADS, tiles, attend)
    else:
        o = attend(prep(q_ref, 0))
    for gi in range(group):
        cols = slice(gi * HEAD_DIM, (gi + 1) * HEAD_DIM)
        y_ref[:, cols] = (o[gi * tq:(gi + 1) * tq] * _silu(ug_ref[:, cols].astype(F32))).astype(BF16)


def _mixer_c(u, uc, q_g, k_g, tables, latent):
    tq = CTX_LEN
    q_src = u if latent else uc
    rows = q_src.shape[0]
    tiles = rows // BATCH // tq
    group = C_HEADS // C_KV_HEADS
    width = group * HEAD_DIM
    n_group = N_KEY_GROUP if latent else 0
    n_keys = KEY_CHUNK + n_group * KEY_GROUP
    k0, v0, gate0 = 1024 // LANES, 1280 // LANES, 2048 // width
    col = lambda off: (lambda b, h, qi: (b, off + h))
    in_specs = [
        pl.BlockSpec((tq, width), lambda b, h, qi: (b * tiles + qi, h)),
        pl.BlockSpec((CTX_LEN, LANES), col(k0)),
        pl.BlockSpec((CTX_LEN, LANES), col(v0)),
        pl.BlockSpec((tq, width), lambda b, h, qi: (b * tiles + qi, gate0 + h)),
        pl.BlockSpec((1, LANES), lambda b, h, qi: (0, 0)),
        pl.BlockSpec((1, LANES), lambda b, h, qi: (0, 0)),
    ]
    args = [q_src, uc, uc, q_src, q_g, k_g]
    if latent:
        table = pl.BlockSpec((SEQ, LANES), lambda b, h, qi: (0, 0))

        def next_q(b, h, qi):
            b, h, qi = _next_step(b, h, qi, C_KV_HEADS, tiles)
            return b * tiles + qi, h
        in_specs += [pl.BlockSpec((SEQ, LANES), col(k0)), pl.BlockSpec((SEQ, LANES), col(v0)),
                     table, table, pl.BlockSpec((tq, width), next_q)]
        args += [u, u, tables["c128"], tables["s128"], u]
    return pl.pallas_call(
        functools.partial(_mixer_c_kernel, latent=latent, tq=tq, tiles=tiles, n_group=n_group),
        grid=(BATCH, C_KV_HEADS, tiles),
        in_specs=in_specs,
        out_specs=pl.BlockSpec((tq, width), lambda b, h, qi: (b * tiles + qi, h)),
        out_shape=jax.ShapeDtypeStruct((rows, C_HEADS * HEAD_DIM), BF16),
        scratch_shapes=[pltpu.VMEM((n_keys, LANES), BF16), pltpu.VMEM((n_keys, 2 * LANES), BF16)]
        + ([pltpu.VMEM((2, group * tq, LANES), BF16)] if latent else [])
        + _attend_scratch(group * tq, n_group),
        compiler_params=_params(3),
        name="mixer_c" if latent else "mixer_c_ctx",
    )(*args)


def _mixer_d_kernel(*refs, latent, tq, n_group):
    if latent:
        (q_ref, knc_ref, vc_ref, pec_ref, ug_ref, knl_ref, vl_ref, pel_ref, tc_ref, tsa_ref,
         tsb_ref, y_ref, ks_ref, vs_ref, *att) = refs
    else:
        q_ref, knc_ref, vc_ref, pec_ref, ug_ref, y_ref, ks_ref, vs_ref, *att = refs
    qi = pl.program_id(2)

    @pl.when(qi == 0)
    def _():
        ks_ref[0:CTX_LEN, 0:D_NOPE] = knc_ref[...]
        ks_ref[0:CTX_LEN, D_NOPE:2 * D_NOPE] = pec_ref[...]
        _store_values(vs_ref, slice(0, CTX_LEN), vc_ref[...])
        if latent:
            def prep(rows, i):
                pe = _rope_quarter(pel_ref[rows, :].astype(F32), tc_ref[rows, :], tsa_ref[rows, :],
                                   tsb_ref[rows, :])
                dst = pl.ds(pl.multiple_of(CTX_LEN + i * PREP_ROWS, CTX_LEN), PREP_ROWS)
                ks_ref[dst, 0:D_NOPE] = knl_ref[rows, :]
                ks_ref[dst, D_NOPE:2 * D_NOPE] = pe.astype(BF16)
                _store_values(vs_ref, dst, vl_ref[rows, :])
            _for_row_blocks(SEQ, PREP_ROWS, prep)

    def prep(src_ref, tile):
        scale = (D_NOPE + D_ROPE) ** -0.5 * LOG2E
        q_nope = src_ref[:, 0:D_NOPE].astype(F32)
        q_pe = src_ref[:, D_NOPE:2 * D_NOPE].astype(F32)
        if latent:
            rows = pl.ds(pl.multiple_of(tile * tq, tq), tq)
            q_pe = _rope_quarter(q_pe, tc_ref[rows, :], tsa_ref[rows, :], tsb_ref[rows, :])
        return jnp.concatenate([(q_nope * scale).astype(BF16), (q_pe * scale).astype(BF16)], axis=1)

    attend = lambda q: _attend(q, ks_ref, vs_ref, att, n_group)
    o = attend(prep(q_ref, qi))
    y_ref[...] = (o * _silu(ug_ref[...].astype(F32))).astype(BF16)


def _mixer_d(u, uc, qd, kvd, qdc, kvdc, tables, latent):
    tq = 1024 if latent else CTX_LEN
    q_src, ug_src = (qd, u) if latent else (qdc, uc)
    rows = q_src.shape[0]
    tiles = rows // BATCH // tq
    n_group = N_KEY_GROUP if latent else 0
    n_keys = KEY_CHUNK + n_group * KEY_GROUP
    pe0, gate0 = 4352 // LANES, (2048 + 1024) // LANES
    in_specs = [
        pl.BlockSpec((tq, 2 * LANES), lambda b, h, qi: (b * tiles + qi, h)),
        pl.BlockSpec((CTX_LEN, LANES), lambda b, h, qi: (b, 2 * h)),
        pl.BlockSpec((CTX_LEN, LANES), lambda b, h, qi: (b, 2 * h + 1)),
        pl.BlockSpec((CTX_LEN, LANES), lambda b, h, qi: (b, pe0)),
        pl.BlockSpec((tq, LANES), lambda b, h, qi: (b * tiles + qi, gate0 + h)),
    ]
    args = [q_src, kvdc, kvdc, uc, ug_src]
    if latent:
        table = pl.BlockSpec((SEQ, LANES), lambda b, h, qi: (0, 0))
        in_specs += [
            pl.BlockSpec((SEQ, LANES), lambda b, h, qi: (b, 2 * h)),
            pl.BlockSpec((SEQ, LANES), lambda b, h, qi: (b, 2 * h + 1)),
            pl.BlockSpec((SEQ, LANES), lambda b, h, qi: (b, pe0)),
            table, table, table]
        args += [kvd, kvd, u, tables["cd"], tables["sda"], tables["sdb"]]
    return pl.pallas_call(
        functools.partial(_mixer_d_kernel, latent=latent, tq=tq, n_group=n_group),
        grid=(BATCH, D_HEADS, tiles),
        in_specs=in_specs,
        out_specs=pl.BlockSpec((tq, LANES), lambda b, h, qi: (b * tiles + qi, h)),
        out_shape=jax.ShapeDtypeStruct((rows, D_HEADS * HEAD_DIM), BF16),
        scratch_shapes=[pltpu.VMEM((n_keys, 2 * LANES), BF16)] * 2
        + _attend_scratch(tq, n_group),
        compiler_params=_params(3),
        name="mixer_d" if latent else "mixer_d_ctx",
    )(*args)


def _rope_tables():
    n = jnp.arange(SEQ)
    row = (n // GRID_W).astype(F32)
    col = (n % GRID_W).astype(F32)

    def cos_sin(dim):
        n_freq = dim // 4
        inv = ROPE_BASE ** (-jnp.arange(n_freq, dtype=F32) / n_freq)
        ang = jnp.concatenate([row[:, None] * inv[None], col[:, None] * inv[None]], axis=-1)
        return jnp.cos(ang), jnp.sin(ang)

    c, s = cos_sin(HEAD_DIM)
    tables = {"c128": jnp.concatenate([c, c], 1), "s128": jnp.concatenate([-s, s], 1)}
    c, s = cos_sin(A_QK_DIM)
    z = jnp.zeros_like(c)
    tables["ca"] = jnp.concatenate([c, c, c, c], 1)
    tables["saa"] = jnp.concatenate([-s, z, -s, z], 1)
    tables["sab"] = jnp.concatenate([z, s, z, s], 1)
    tables["cd"] = jnp.concatenate([c, c, z, z], 1)
    tables["sda"] = jnp.concatenate([-s, z, z, z], 1)
    tables["sdb"] = jnp.concatenate([z, s, z, z], 1)
    return tables


def _odd_weights(w):
    w = w.astype(BF16)
    d0 = C_IN
    gate0 = C_IN + D_IN
    pieces = [w[..., :d0 + D_Q_RANK], w[..., gate0:], w[..., d0 + D_Q_RANK:gate0]]
    used = sum(p.shape[-1] for p in pieces)
    pieces.append(jnp.zeros(w.shape[:-1] + (ODD_PAD - used,), BF16))
    return jnp.concatenate(pieces, axis=-1)


def kernel(x, c, ctx, c_ctx, w_mod, b_mod, norm_g, w_o, final_g, e_w_in, a_lam_q1, a_lam_k1, a_lam_q2, a_lam_k2, a_sub_g, b_sink, o_w_in, c_q_g, c_k_g, d_q_a_g, d_kv_a_g, d_w_q_b, d_w_kv_b):
    xl = x.reshape(N_LAT_ROWS, D_MODEL)
    xc = ctx.reshape(N_CTX_ROWS, D_MODEL)
    cc = jnp.concatenate([c, c_ctx[None], jnp.zeros((8 - BATCH - 1, D_MODEL), F32)], axis=0)
    mod = _modulation(cc, w_mod, b_mod)
    tables = _rope_tables()
    w_out = w_o.astype(BF16)
    w_even = e_w_in.astype(BF16)
    w_odd = _odd_weights(o_w_in)

    for l in range(DEPTH):
        with_ctx = l < DEPTH - 1
        i = l // 2
        shift, scale, gate = (mod[l, :BATCH, k * D_MODEL:(k + 1) * D_MODEL][:, None, :]
                              for k in range(3))
        cshift, cscale, cgate = (mod[l, BATCH:BATCH + 1, k * D_MODEL:(k + 1) * D_MODEL][:, None, :]
                                 for k in range(3))
        g = norm_g[l][None]
        if l % 2 == 0:
            u = _in_proj(xl, scale, shift, g, w_even, i, "in_proj_even")
            uc = _in_proj(xc, cscale, cshift, g, w_even, i, "in_proj_even_ctx")
            lam_init = 0.8 - 0.6 * math.exp(-0.3 * l)
            lam_vecs = jnp.stack([a_lam_q1[i], a_lam_k1[i], a_lam_q2[i], a_lam_k2[i]]).astype(F32)
            sub_g = a_sub_g[i][None]
            y1 = _mixer_a(u, uc, lam_vecs, sub_g, tables, lam_init, True)
            y2 = _mixer_b(u, uc, b_sink[i], tables, True)
            if with_ctx:
                yc1 = _mixer_a(u, uc, lam_vecs, sub_g, tables, lam_init, False)
                yc2 = _mixer_b(u, uc, b_sink[i], tables, False)
        else:
            u = _in_proj(xl, scale, shift, g, w_odd, i, "in_proj_odd")
            uc = _in_proj(xc, cscale, cshift, g, w_odd, i, "in_proj_odd_ctx")
            wq = jnp.pad(d_w_q_b[i].reshape(D_Q_RANK, D_HEADS, D_NOPE + D_ROPE),
                         ((0, 0), (0, 0), (0, 2 * LANES - D_NOPE - D_ROPE)))
            wq = wq.reshape(D_Q_RANK, D_HEADS * 2 * LANES).astype(BF16)
            wkv = d_w_kv_b[i].astype(BF16)
            qd, kvd = _dproj(u, d_q_a_g[i][None], d_kv_a_g[i][None], wq, wkv, "dproj")
            qdc, kvdc = _dproj(uc, d_q_a_g[i][None], d_kv_a_g[i][None], wq, wkv, "dproj_ctx")
            y1 = _mixer_c(u, uc, c_q_g[i][None], c_k_g[i][None], tables, True)
            y2 = _mixer_d(u, uc, qd, kvd, qdc, kvdc, tables, True)
            if with_ctx:
                yc1 = _mixer_c(u, uc, c_q_g[i][None], c_k_g[i][None], tables, False)
                yc2 = _mixer_d(u, uc, qd, kvd, qdc, kvdc, tables, False)
        last = l == DEPTH - 1
        xl = _out_proj(xl, y1, y2, w_out, l, gate, final_g[None] if last else None,
                       "out_proj_final" if last else "out_proj")
        if with_ctx:
            xc = _out_proj(xc, yc1, yc2, w_out, l, cgate, None, "out_proj_ctx")
    return xl.reshape(BATCH, SEQ, D_MODEL)
```

```python
import functools
import math

import jax
import jax.numpy as jnp
from jax import lax
from jax.experimental import pallas as pl
from jax.experimental.pallas import tpu as pltpu

F32 = jnp.float32
BF16 = jnp.bfloat16

D_MODEL = 2048
BATCH = 2
SEQ = 4096
DEPTH = 4
GRID_W = 64
CTX_LEN = 256
HEAD_DIM = 128
ROPE_BASE = 10000.0
EPS = 1e-6
NEG_INF = -1e30
WINDOW = 128

A_HEADS = 8
A_QK_DIM = 64
B_HEADS = 8
B_KV_HEADS = 2
C_HEADS = 8
C_KV_HEADS = 2
D_HEADS = 8
D_Q_RANK = 512
D_KV_RANK = 256
D_NOPE = 128
D_ROPE = 64

A_IN = 3072
B_IN = 1536
C_IN = 1536
D_IN = 832
MIX_WIDTH = 2048
EVEN_IN = A_IN + B_IN + MIX_WIDTH
ODD_IN = C_IN + D_IN + MIX_WIDTH
ODD_PAD = 4608

LOG2E = math.log2(math.e)
LANES = 128
KEY_CHUNK = 256
KEY_GROUP = 1024
N_KEY_GROUP = SEQ // KEY_GROUP
N_LAT_ROWS = BATCH * SEQ
N_CTX_ROWS = BATCH * CTX_LEN
VMEM_LIMIT = 48 * 1024 * 1024


def _params(n_grid_dims, vmem=VMEM_LIMIT):
    return pltpu.CompilerParams(
        dimension_semantics=("arbitrary",) * n_grid_dims, vmem_limit_bytes=vmem)


def _silu(v):
    return v * (1.0 / (1.0 + jnp.exp(-v)))


def _rms(v, g):
    return v * lax.rsqrt(jnp.mean(v * v, axis=-1, keepdims=True) + EPS) * g


def _mod_kernel(cc_ref, w_ref, b_ref, o_ref):
    s = _silu(cc_ref[...]).astype(BF16)
    o_ref[0] = jnp.dot(s, w_ref[0].astype(BF16), preferred_element_type=F32) + b_ref[0]


def _modulation(cc, w_mod, b_mod):
    tn = 1024
    return pl.pallas_call(
        _mod_kernel,
        grid=(DEPTH, 3 * D_MODEL // tn),
        in_specs=[
            pl.BlockSpec((8, D_MODEL), lambda l, j: (0, 0)),
            pl.BlockSpec((1, D_MODEL, tn), lambda l, j: (l, 0, j)),
            pl.BlockSpec((1, 1, tn), lambda l, j: (l, 0, j)),
        ],
        out_specs=pl.BlockSpec((1, 8, tn), lambda l, j: (l, 0, j)),
        out_shape=jax.ShapeDtypeStruct((DEPTH, 8, 3 * D_MODEL), F32),
        compiler_params=_params(2),
        name="modulation",
    )(cc, w_mod, b_mod.reshape(DEPTH, 1, 3 * D_MODEL))


def _in_proj_kernel(x0_ref, sc0_ref, sh0_ref, xn_ref, scn_ref, shn_ref, g_ref, w_ref, o_ref,
                    ha_ref, hb_ref, *, n_col_tiles):
    i = pl.program_id(0)
    j = pl.program_id(1)

    def normed(x_ref, sc_ref, sh_ref):
        return (_rms(x_ref[...], g_ref[...]) * (1.0 + sc_ref[0]) + sh_ref[0]).astype(BF16)

    @pl.when((i == 0) & (j == 0))
    def _():
        ha_ref[...] = normed(x0_ref, sc0_ref, sh0_ref)

    for parity, (cur_ref, nxt_ref) in enumerate(((ha_ref, hb_ref), (hb_ref, ha_ref))):
        @pl.when((i % 2 == parity) & (j < n_col_tiles - 1))
        def _():
            o_ref[...] = jnp.dot(cur_ref[...], w_ref[...], preferred_element_type=F32).astype(BF16)

        @pl.when((i % 2 == parity) & (j == n_col_tiles - 1))
        def _():
            nxt_ref[...] = normed(xn_ref, scn_ref, shn_ref)
            o_ref[...] = jnp.dot(cur_ref[...], w_ref[...], preferred_element_type=F32).astype(BF16)


def _in_proj(x, scale, shift, g, w, layer, name):
    rows, n_out = x.shape[0], w.shape[2]
    tm = 512
    tn = n_out // (4 if n_out == EVEN_IN else 3)
    n_row_tiles = rows // tm
    tiles_per_mod = n_row_tiles // scale.shape[0]
    nxt = lambda i: jnp.minimum(i + 1, n_row_tiles - 1)
    mod_spec = lambda row_tile: pl.BlockSpec(
        (1, 1, D_MODEL), lambda i, j: (row_tile(i) // tiles_per_mod, 0, 0))
    first = lambda i: 0
    return pl.pallas_call(
        functools.partial(_in_proj_kernel, n_col_tiles=n_out // tn),
        grid=(n_row_tiles, n_out // tn),
        in_specs=[
            pl.BlockSpec((tm, D_MODEL), lambda i, j: (0, 0)),
            mod_spec(first),
            mod_spec(first),
            pl.BlockSpec((tm, D_MODEL), lambda i, j: (nxt(i), 0)),
            mod_spec(nxt),
            mod_spec(nxt),
            pl.BlockSpec((1, D_MODEL), lambda i, j: (0, 0)),
            pl.BlockSpec((None, D_MODEL, tn), lambda i, j: (layer, 0, j)),
        ],
        out_specs=pl.BlockSpec((tm, tn), lambda i, j: (i, j)),
        out_shape=jax.ShapeDtypeStruct((rows, n_out), BF16),
        scratch_shapes=[pltpu.VMEM((tm, D_MODEL), BF16)] * 2,
        compiler_params=_params(2),
        name=name,
    )(x, scale, shift, x, scale, shift, g, w)


def _out_proj_kernel(*refs, final):
    if final:
        x_ref, y1_ref, y2_ref, w_ref, gate_ref, fg_ref, o_ref = refs
    else:
        x_ref, y1_ref, y2_ref, w_ref, gate_ref, o_ref = refs
    half = MIX_WIDTH // 2
    d = jnp.dot(y1_ref[...], w_ref[0:half, :], preferred_element_type=F32)
    d = d + jnp.dot(y2_ref[...], w_ref[half:MIX_WIDTH, :], preferred_element_type=F32)
    xn = x_ref[...] + gate_ref[0] * d
    if final:
        xn = _rms(xn, fg_ref[...])
    o_ref[...] = xn


def _out_proj(x, y1, y2, w, layer, gate, final_g, name):
    rows = x.shape[0]
    tm = min(rows, 1024)
    tiles_per_mod = rows // tm // gate.shape[0]
    half = MIX_WIDTH // 2
    in_specs = [
        pl.BlockSpec((tm, D_MODEL), lambda i: (i, 0)),
        pl.BlockSpec((tm, half), lambda i: (i, 0)),
        pl.BlockSpec((tm, half), lambda i: (i, 0)),
        pl.BlockSpec((None, MIX_WIDTH, D_MODEL), lambda i: (layer, 0, 0),
                     pipeline_mode=pl.Buffered(1)),
        pl.BlockSpec((1, 1, D_MODEL), lambda i: (i // tiles_per_mod, 0, 0)),
    ]
    args = [x, y1, y2, w, gate]
    if final_g is not None:
        in_specs.append(pl.BlockSpec((1, D_MODEL), lambda i: (0, 0)))
        args.append(final_g)
    return pl.pallas_call(
        functools.partial(_out_proj_kernel, final=final_g is not None),
        grid=(rows // tm,),
        in_specs=in_specs,
        out_specs=pl.BlockSpec((tm, D_MODEL), lambda i: (i, 0)),
        out_shape=jax.ShapeDtypeStruct((rows, D_MODEL), F32),
        compiler_params=_params(1, 62 * 1024 * 1024),
        name=name,
    )(*args)


def _dproj_kernel(cq_ref, ckv_ref, qg_ref, kvg_ref, wq_ref, wkv_ref, q_out, kv_out):
    cq = _rms(cq_ref[...].astype(F32), qg_ref[...]).astype(BF16)
    ckv = _rms(ckv_ref[...].astype(F32), kvg_ref[...]).astype(BF16)
    q_out[...] = jnp.dot(cq, wq_ref[...], preferred_element_type=F32).astype(BF16)
    kv_out[...] = jnp.dot(ckv, wkv_ref[...], preferred_element_type=F32).astype(BF16)


def _dproj(u, q_g, kv_g, wq, wkv, name):
    rows = u.shape[0]
    tm = 512
    n_q, n_kv = wq.shape[1], wkv.shape[1]
    return pl.pallas_call(
        _dproj_kernel,
        grid=(rows // tm,),
        in_specs=[
            pl.BlockSpec((tm, D_Q_RANK), lambda i: (i, 1536 // D_Q_RANK)),
            pl.BlockSpec((tm, D_KV_RANK), lambda i: (i, 4096 // D_KV_RANK)),
            pl.BlockSpec((1, D_Q_RANK), lambda i: (0, 0)),
            pl.BlockSpec((1, D_KV_RANK), lambda i: (0, 0)),
            pl.BlockSpec((D_Q_RANK, n_q), lambda i: (0, 0)),
            pl.BlockSpec((D_KV_RANK, n_kv), lambda i: (0, 0)),
        ],
        out_specs=[
            pl.BlockSpec((tm, n_q), lambda i: (i, 0)),
            pl.BlockSpec((tm, n_kv), lambda i: (i, 0)),
        ],
        out_shape=[
            jax.ShapeDtypeStruct((rows, n_q), BF16),
            jax.ShapeDtypeStruct((rows, n_kv), BF16),
        ],
        compiler_params=_params(1),
        name=name,
    )(u, u, q_g, kv_g, wq, wkv)


def _rope_half(x, c, s):
    return x * c + pltpu.roll(x, 64, 1) * s


def _rope_quarter(x, c, sa, sb):
    return x * c + pltpu.roll(x, 96, 1) * sa + pltpu.roll(x, 32, 1) * sb


def _attend(q, k_ref, v_ref, att, n_group):
    sc_ref, mb_ref = att[0], att[1]
    sl_ref = att[2] if n_group else None
    half = q.shape[0] // 2
    parts = [slice(0, half), slice(half, 2 * half)]
    trans_b = (((1,), (1,)), ((), ()))

    def lane_fold_max(s):
        parts = [s[:, j * LANES:(j + 1) * LANES] for j in range(s.shape[1] // LANES)]
        while len(parts) > 1:
            odd = [parts[-1]] if len(parts) % 2 else []
            parts = [jnp.maximum(a, b) for a, b in zip(parts[0::2], parts[1::2])] + odd
        return parts[0]

    def group_rows(g):
        return slice(KEY_CHUNK + g * KEY_GROUP, KEY_CHUNK + (g + 1) * KEY_GROUP)

    ms = []
    for r in parts:
        s = lax.dot_general(q[r], k_ref[0:KEY_CHUNK, :], trans_b, preferred_element_type=F32)
        sc_ref[r, :] = s
        ms.append(lane_fold_max(s))
    for g in range(n_group):
        for i, r in enumerate(parts):
            s = lax.dot_general(q[r], k_ref[group_rows(g), :], trans_b,
                                preferred_element_type=F32)
            sl_ref[g, r, :] = s
            ms[i] = jnp.maximum(ms[i], lane_fold_max(s))
    for m, r in zip(ms, parts):
        mb_ref[r, :] = jnp.broadcast_to(jnp.max(m, axis=1, keepdims=True), (half, LANES))

    def weights(s, mb):
        cols = [jnp.exp2(s[:, j * LANES:(j + 1) * LANES] - mb) for j in range(s.shape[1] // LANES)]
        return jnp.concatenate(cols, axis=1).astype(BF16)

    accs = [jnp.dot(weights(sc_ref[r, :], mb_ref[r, :]), v_ref[0:KEY_CHUNK, :],
                    preferred_element_type=F32) for r in parts]
    for g in range(n_group):
        for i, r in enumerate(parts):
            accs[i] = accs[i] + jnp.dot(weights(sl_ref[g, r, :], mb_ref[r, :]),
                                        v_ref[group_rows(g), :], preferred_element_type=F32)
    return jnp.concatenate([a[:, :LANES] * (1.0 / a[:, LANES:]) for a in accs], axis=0)


def _attend_scratch(m_rows, n_group):
    scratch = [pltpu.VMEM((m_rows, KEY_CHUNK), F32), pltpu.VMEM((m_rows, LANES), F32)]
    if n_group:
        scratch.append(pltpu.VMEM((n_group, m_rows, KEY_GROUP), F32))
    return scratch


def _store_values(vs_ref, rows, v):
    vs_ref[rows, 0:LANES] = v
    vs_ref[rows, LANES:2 * LANES] = jnp.ones(v.shape, BF16)


def _for_row_blocks(n_rows, block, fn):
    def body(i, carry):
        fn(pl.ds(pl.multiple_of(i * block, block), block), i)
        return carry
    lax.fori_loop(0, n_rows // block, body, 0)


PREP_ROWS = 512


def _next_step(b, h, qi, n_heads, tiles):
    t = jnp.minimum((b * n_heads + h) * tiles + qi + 1, BATCH * n_heads * tiles - 1)
    return t // (n_heads * tiles), (t // tiles) % n_heads, t % tiles


def _with_queries_ahead(prep, q_ref, qn_ref, qs_ref, n_heads, tiles, attend):
    b, h, qi = pl.program_id(0), pl.program_id(1), pl.program_id(2)
    step = (b * n_heads + h) * tiles + qi
    slot = step % 2

    @pl.when(step == 0)
    def _():
        qs_ref[0] = prep(q_ref, qi)

    out = attend(qs_ref[slot])
    qs_ref[1 - slot] = prep(qn_ref, _next_step(b, h, qi, n_heads, tiles)[2])
    return out


def _mixer_a_kernel(*refs, latent, lam_init, tq, n_group):
    if latent:
        (q_ref, kc_ref, vc_ref, ug_ref, lam_ref, subg_ref, kl_ref, vl_ref, tc_ref, tsa_ref,
         tsb_ref, y_ref, ks_ref, vs_ref, *att) = refs
    else:
        q_ref, kc_ref, vc_ref, ug_ref, lam_ref, subg_ref, y_ref, ks_ref, vs_ref, *att = refs
    qi = pl.program_id(2)

    @pl.when(qi == 0)
    def _():
        ks_ref[0:CTX_LEN, :] = kc_ref[...]
        _store_values(vs_ref, slice(0, CTX_LEN), vc_ref[...])
        if latent:
            def prep(rows, i):
                k = _rope_quarter(kl_ref[rows, :].astype(F32), tc_ref[rows, :], tsa_ref[rows, :],
                                  tsb_ref[rows, :])
                dst = pl.ds(pl.multiple_of(CTX_LEN + i * PREP_ROWS, CTX_LEN), PREP_ROWS)
                ks_ref[dst, :] = k.astype(BF16)
                _store_values(vs_ref, dst, vl_ref[rows, :])
            _for_row_blocks(SEQ, PREP_ROWS, prep)

    def prep(src_ref, tile):
        q = src_ref[...].astype(F32)
        if latent:
            rows = pl.ds(pl.multiple_of(tile * tq, tq), tq)
            q = _rope_quarter(q, tc_ref[rows, :], tsa_ref[rows, :], tsb_ref[rows, :])
        q = q * (A_QK_DIM ** -0.5 * LOG2E)
        lane = lax.broadcasted_iota(jnp.int32, (1, LANES), 1)
        lo = (lane < A_QK_DIM).astype(F32)
        return jnp.concatenate([(q * lo).astype(BF16), (q * (1.0 - lo)).astype(BF16)], axis=0)

    attend = lambda qs: _attend(qs, ks_ref, vs_ref, att, n_group)
    o = attend(prep(q_ref, qi))
    lv = lam_ref[...]
    lam = (jnp.exp(jnp.sum(lv[0:1] * lv[1:2], axis=1, keepdims=True))
           - jnp.exp(jnp.sum(lv[2:3] * lv[3:4], axis=1, keepdims=True)) + lam_init)
    o = o[0:tq] - lam * o[tq:2 * tq]
    y = _rms(o, subg_ref[...]) * (1.0 - lam_init)
    y_ref[...] = (y * _silu(ug_ref[...].astype(F32))).astype(BF16)


def _mixer_a(u, uc, lam_vecs, sub_g, tables, lam_init, latent):
    tq = 512 if latent else CTX_LEN
    q_src = u if latent else uc
    rows = q_src.shape[0]
    tiles = rows // BATCH // tq
    n_group = N_KEY_GROUP if latent else 0
    n_keys = KEY_CHUNK + n_group * KEY_GROUP
    gate0 = (A_IN + B_IN) // LANES
    col = lambda off: (lambda b, h, qi: (b, off + h))
    in_specs = [
        pl.BlockSpec((tq, LANES), lambda b, h, qi: (b * tiles + qi, h)),
        pl.BlockSpec((CTX_LEN, LANES), col(8)),
        pl.BlockSpec((CTX_LEN, LANES), col(16)),
        pl.BlockSpec((tq, LANES), lambda b, h, qi: (b * tiles + qi, gate0 + h)),
        pl.BlockSpec((4, A_QK_DIM), lambda b, h, qi: (0, 0)),
        pl.BlockSpec((1, LANES), lambda b, h, qi: (0, 0)),
    ]
    args = [q_src, uc, uc, q_src, lam_vecs, sub_g]
    if latent:
        table = pl.BlockSpec((SEQ, LANES), lambda b, h, qi: (0, 0))
        in_specs += [pl.BlockSpec((SEQ, LANES), col(8)), pl.BlockSpec((SEQ, LANES), col(16)),
                     table, table, table]
        args += [u, u, tables["ca"], tables["saa"], tables["sab"]]
    return pl.pallas_call(
        functools.partial(_mixer_a_kernel, latent=latent, lam_init=lam_init, tq=tq,
                          n_group=n_group),
        grid=(BATCH, A_HEADS, tiles),
        in_specs=in_specs,
        out_specs=pl.BlockSpec((tq, LANES), lambda b, h, qi: (b * tiles + qi, h)),
        out_shape=jax.ShapeDtypeStruct((rows, A_HEADS * HEAD_DIM), BF16),
        scratch_shapes=[pltpu.VMEM((n_keys, LANES), BF16), pltpu.VMEM((n_keys, 2 * LANES), BF16)]
        + _attend_scratch(2 * tq, n_group),
        compiler_params=_params(3),
        name="mixer_a" if latent else "mixer_a_ctx",
    )(*args)


def _mixer_b_kernel(*refs, latent, tq):
    if latent:
        (sink_ref, q_ref, kc_ref, vc_ref, ug_ref, kl_ref, vl_ref, tc_ref, ts_ref, y_ref,
         ks_ref) = refs
    else:
        sink_ref, q_ref, kc_ref, vc_ref, ug_ref, y_ref = refs
    kvh = pl.program_id(1)
    qi = pl.program_id(2)
    group = B_HEADS // B_KV_HEADS
    span = tq + 2 * WINDOW

    if latent:
        @pl.when(qi == 0)
        def _():
            def prep(rows, i):
                ks_ref[rows, :] = _rope_half(kl_ref[rows, :].astype(F32), tc_ref[rows, :],
                                             ts_ref[rows, :]).astype(BF16)
            _for_row_blocks(SEQ, PREP_ROWS, prep)

        rows = pl.ds(pl.multiple_of(qi * tq, tq), tq)
        start = pl.multiple_of(jnp.clip(qi * tq - WINDOW, 0, SEQ - span), WINDOW)
        k_win = ks_ref[pl.ds(start, span), :]
        v_win = vl_ref[pl.ds(start, span), :]
        q_pos = qi * tq + lax.broadcasted_iota(jnp.int32, (tq, span), 0)
        k_pos = start + lax.broadcasted_iota(jnp.int32, (tq, span), 1)
        in_window = jnp.abs(k_pos - q_pos) <= WINDOW

    def lane_fold_max(s):
        m = s[:, 0:LANES]
        for j in range(1, s.shape[1] // LANES):
            m = jnp.maximum(m, s[:, j * LANES:(j + 1) * LANES])
        return m

    kc = kc_ref[...]
    vc = jnp.concatenate([vc_ref[...], jnp.ones((CTX_LEN, LANES), BF16)], axis=1)
    if latent:
        v_win = jnp.concatenate([v_win, jnp.ones((span, LANES), BF16)], axis=1)
    trans_b = (((1,), (1,)), ((), ()))
    for gi in range(group):
        cols = slice(gi * HEAD_DIM, (gi + 1) * HEAD_DIM)
        q = q_ref[:, cols].astype(F32)
        if latent:
            q = _rope_half(q, tc_ref[rows, :], ts_ref[rows, :])
        q = (q * (HEAD_DIM ** -0.5 * LOG2E)).astype(BF16)
        sink = sink_ref[kvh * group + gi] * LOG2E
        s_c = lax.dot_general(q, kc, trans_b, preferred_element_type=F32)
        mf = lane_fold_max(s_c)
        if latent:
            s_w = lax.dot_general(q, k_win, trans_b, preferred_element_type=F32)
            s_w = jnp.where(in_window, s_w, NEG_INF)
            mf = jnp.maximum(mf, lane_fold_max(s_w))
        m = jnp.maximum(jnp.max(mf, axis=1, keepdims=True), sink)
        acc = jnp.dot(jnp.exp2(s_c - m).astype(BF16), vc, preferred_element_type=F32)
        if latent:
            acc = acc + jnp.dot(jnp.exp2(s_w - m).astype(BF16), v_win,
                                preferred_element_type=F32)
        den = acc[:, LANES:] + jnp.exp2(sink - m)
        o = acc[:, :LANES] * (1.0 / den)
        y_ref[:, cols] = (o * _silu(ug_ref[:, cols].astype(F32))).astype(BF16)


def _mixer_b(u, uc, sink, tables, latent):
    tq = 256
    q_src = u if latent else uc
    rows = q_src.shape[0]
    tiles = rows // BATCH // tq
    width = (B_HEADS // B_KV_HEADS) * HEAD_DIM
    q0, k0, v0 = A_IN // width, (A_IN + 1024) // LANES, (A_IN + 1280) // LANES
    gate0 = (A_IN + B_IN + 1024) // width
    col = lambda off: (lambda b, h, qi: (b, off + h))
    in_specs = [
        pl.BlockSpec(memory_space=pltpu.SMEM),
        pl.BlockSpec((tq, width), lambda b, h, qi: (b * tiles + qi, q0 + h)),
        pl.BlockSpec((CTX_LEN, LANES), col(k0)),
        pl.BlockSpec((CTX_LEN, LANES), col(v0)),
        pl.BlockSpec((tq, width), lambda b, h, qi: (b * tiles + qi, gate0 + h)),
    ]
    args = [sink, q_src, uc, uc, q_src]
    scratch = []
    if latent:
        table = pl.BlockSpec((SEQ, LANES), lambda b, h, qi: (0, 0))
        in_specs += [pl.BlockSpec((SEQ, LANES), col(k0)), pl.BlockSpec((SEQ, LANES), col(v0)),
                     table, table]
        args += [u, u, tables["c128"], tables["s128"]]
        scratch = [pltpu.VMEM((SEQ, LANES), BF16)]
    return pl.pallas_call(
        functools.partial(_mixer_b_kernel, latent=latent, tq=tq),
        grid=(BATCH, B_KV_HEADS, tiles),
        in_specs=in_specs,
        out_specs=pl.BlockSpec((tq, width), lambda b, h, qi: (b * tiles + qi, h)),
        out_shape=jax.ShapeDtypeStruct((rows, B_HEADS * HEAD_DIM), BF16),
        scratch_shapes=scratch,
        compiler_params=_params(3),
        name="mixer_b" if latent else "mixer_b_ctx",
    )(*args)


def _mixer_c_kernel(*refs, latent, tq, tiles, n_group):
    if latent:
        (q_ref, kc_ref, vc_ref, ug_ref, qg_ref, kg_ref, kl_ref, vl_ref, tc_ref, ts_ref, qn_ref,
         y_ref, ks_ref, vs_ref, qs_ref, *att) = refs
    else:
        q_ref, kc_ref, vc_ref, ug_ref, qg_ref, kg_ref, y_ref, ks_ref, vs_ref, *att = refs
    qi = pl.program_id(2)
    group = C_HEADS // C_KV_HEADS

    @pl.when(qi == 0)
    def _():
        ks_ref[0:CTX_LEN, :] = _rms(kc_ref[...].astype(F32), kg_ref[...]).astype(BF16)
        _store_values(vs_ref, slice(0, CTX_LEN), vc_ref[...])
        if latent:
            def prep(rows, i):
                k = _rms(kl_ref[rows, :].astype(F32), kg_ref[...])
                k = _rope_half(k, tc_ref[rows, :], ts_ref[rows, :])
                dst = pl.ds(pl.multiple_of(CTX_LEN + i * PREP_ROWS, CTX_LEN), PREP_ROWS)
                ks_ref[dst, :] = k.astype(BF16)
                _store_values(vs_ref, dst, vl_ref[rows, :])
            _for_row_blocks(SEQ, PREP_ROWS, prep)

    def prep(src_ref, tile):
        if latent:
            rows = pl.ds(pl.multiple_of(tile * tq, tq), tq)
        qs = []
        for gi in range(group):
            q = _rms(src_ref[:, gi * HEAD_DIM:(gi + 1) * HEAD_DIM].astype(F32), qg_ref[...])
            if latent:
                q = _rope_half(q, tc_ref[rows, :], ts_ref[rows, :])
            qs.append((q * (HEAD_DIM ** -0.5 * LOG2E)).astype(BF16))
        return jnp.concatenate(qs, axis=0)

    attend = lambda qs: _attend(qs, ks_ref, vs_ref, att, n_group)
    if latent:
        o = _with_queries_ahead(prep, q_ref, qn_ref, qs_ref, C_KV_HEADS, tiles, attend)
    else:
        o = attend(prep(q_ref, 0))
    for gi in range(group):
        cols = slice(gi * HEAD_DIM, (gi + 1) * HEAD_DIM)
        y_ref[:, cols] = (o[gi * tq:(gi + 1) * tq] * _silu(ug_ref[:, cols].astype(F32))).astype(BF16)


def _mixer_c(u, uc, q_g, k_g, tables, latent):
    tq = CTX_LEN
    q_src = u if latent else uc
    rows = q_src.shape[0]
    tiles = rows // BATCH // tq
    group = C_HEADS // C_KV_HEADS
    width = group * HEAD_DIM
    n_group = N_KEY_GROUP if latent else 0
    n_keys = KEY_CHUNK + n_group * KEY_GROUP
    k0, v0, gate0 = 1024 // LANES, 1280 // LANES, 2048 // width
    col = lambda off: (lambda b, h, qi: (b, off + h))
    in_specs = [
        pl.BlockSpec((tq, width), lambda b, h, qi: (b * tiles + qi, h)),
        pl.BlockSpec((CTX_LEN, LANES), col(k0)),
        pl.BlockSpec((CTX_LEN, LANES), col(v0)),
        pl.BlockSpec((tq, width), lambda b, h, qi: (b * tiles + qi, gate0 + h)),
        pl.BlockSpec((1, LANES), lambda b, h, qi: (0, 0)),
        pl.BlockSpec((1, LANES), lambda b, h, qi: (0, 0)),
    ]
    args = [q_src, uc, uc, q_src, q_g, k_g]
    if latent:
        table = pl.BlockSpec((SEQ, LANES), lambda b, h, qi: (0, 0))

        def next_q(b, h, qi):
            b, h, qi = _next_step(b, h, qi, C_KV_HEADS, tiles)
            return b * tiles + qi, h
        in_specs += [pl.BlockSpec((SEQ, LANES), col(k0)), pl.BlockSpec((SEQ, LANES), col(v0)),
                     table, table, pl.BlockSpec((tq, width), next_q)]
        args += [u, u, tables["c128"], tables["s128"], u]
    return pl.pallas_call(
        functools.partial(_mixer_c_kernel, latent=latent, tq=tq, tiles=tiles, n_group=n_group),
        grid=(BATCH, C_KV_HEADS, tiles),
        in_specs=in_specs,
        out_specs=pl.BlockSpec((tq, width), lambda b, h, qi: (b * tiles + qi, h)),
        out_shape=jax.ShapeDtypeStruct((rows, C_HEADS * HEAD_DIM), BF16),
        scratch_shapes=[pltpu.VMEM((n_keys, LANES), BF16), pltpu.VMEM((n_keys, 2 * LANES), BF16)]
        + ([pltpu.VMEM((2, group * tq, LANES), BF16)] if latent else [])
        + _attend_scratch(group * tq, n_group),
        compiler_params=_params(3),
        name="mixer_c" if latent else "mixer_c_ctx",
    )(*args)


def _mixer_d_kernel(*refs, latent, tq, n_group):
    if latent:
        (q_ref, knc_ref, vc_ref, pec_ref, ug_ref, knl_ref, vl_ref, pel_ref, tc_ref, tsa_ref,
         tsb_ref, y_ref, ks_ref, vs_ref, *att) = refs
    else:
        q_ref, knc_ref, vc_ref, pec_ref, ug_ref, y_ref, ks_ref, vs_ref, *att = refs
    qi = pl.program_id(2)

    @pl.when(qi == 0)
    def _():
        ks_ref[0:CTX_LEN, 0:D_NOPE] = knc_ref[...]
        ks_ref[0:CTX_LEN, D_NOPE:2 * D_NOPE] = pec_ref[...]
        _store_values(vs_ref, slice(0, CTX_LEN), vc_ref[...])
        if latent:
            def prep(rows, i):
                pe = _rope_quarter(pel_ref[rows, :].astype(F32), tc_ref[rows, :], tsa_ref[rows, :],
                                   tsb_ref[rows, :])
                dst = pl.ds(pl.multiple_of(CTX_LEN + i * PREP_ROWS, CTX_LEN), PREP_ROWS)
                ks_ref[dst, 0:D_NOPE] = knl_ref[rows, :]
                ks_ref[dst, D_NOPE:2 * D_NOPE] = pe.astype(BF16)
                _store_values(vs_ref, dst, vl_ref[rows, :])
            _for_row_blocks(SEQ, PREP_ROWS, prep)

    def prep(src_ref, tile):
        scale = (D_NOPE + D_ROPE) ** -0.5 * LOG2E
        q_nope = src_ref[:, 0:D_NOPE].astype(F32)
        q_pe = src_ref[:, D_NOPE:2 * D_NOPE].astype(F32)
        if latent:
            rows = pl.ds(pl.multiple_of(tile * tq, tq), tq)
            q_pe = _rope_quarter(q_pe, tc_ref[rows, :], tsa_ref[rows, :], tsb_ref[rows, :])
        return jnp.concatenate([(q_nope * scale).astype(BF16), (q_pe * scale).astype(BF16)], axis=1)

    attend = lambda q: _attend(q, ks_ref, vs_ref, att, n_group)
    o = attend(prep(q_ref, qi))
    y_ref[...] = (o * _silu(ug_ref[...].astype(F32))).astype(BF16)


def _mixer_d(u, uc, qd, kvd, qdc, kvdc, tables, latent):
    tq = 1024 if latent else CTX_LEN
    q_src, ug_src = (qd, u) if latent else (qdc, uc)
    rows = q_src.shape[0]
    tiles = rows // BATCH // tq
    n_group = N_KEY_GROUP if latent else 0
    n_keys = KEY_CHUNK + n_group * KEY_GROUP
    pe0, gate0 = 4352 // LANES, (2048 + 1024) // LANES
    in_specs = [
        pl.BlockSpec((tq, 2 * LANES), lambda b, h, qi: (b * tiles + qi, h)),
        pl.BlockSpec((CTX_LEN, LANES), lambda b, h, qi: (b, 2 * h)),
        pl.BlockSpec((CTX_LEN, LANES), lambda b, h, qi: (b, 2 * h + 1)),
        pl.BlockSpec((CTX_LEN, LANES), lambda b, h, qi: (b, pe0)),
        pl.BlockSpec((tq, LANES), lambda b, h, qi: (b * tiles + qi, gate0 + h)),
    ]
    args = [q_src, kvdc, kvdc, uc, ug_src]
    if latent:
        table = pl.BlockSpec((SEQ, LANES), lambda b, h, qi: (0, 0))
        in_specs += [
            pl.BlockSpec((SEQ, LANES), lambda b, h, qi: (b, 2 * h)),
            pl.BlockSpec((SEQ, LANES), lambda b, h, qi: (b, 2 * h + 1)),
            pl.BlockSpec((SEQ, LANES), lambda b, h, qi: (b, pe0)),
            table, table, table]
        args += [kvd, kvd, u, tables["cd"], tables["sda"], tables["sdb"]]
    return pl.pallas_call(
        functools.partial(_mixer_d_kernel, latent=latent, tq=tq, n_group=n_group),
        grid=(BATCH, D_HEADS, tiles),
        in_specs=in_specs,
        out_specs=pl.BlockSpec((tq, LANES), lambda b, h, qi: (b * tiles + qi, h)),
        out_shape=jax.ShapeDtypeStruct((rows, D_HEADS * HEAD_DIM), BF16),
        scratch_shapes=[pltpu.VMEM((n_keys, 2 * LANES), BF16)] * 2
        + _attend_scratch(tq, n_group),
        compiler_params=_params(3),
        name="mixer_d" if latent else "mixer_d_ctx",
    )(*args)


def _rope_tables():
    n = jnp.arange(SEQ)
    row = (n // GRID_W).astype(F32)
    col = (n % GRID_W).astype(F32)

    def cos_sin(dim):
        n_freq = dim // 4
        inv = ROPE_BASE ** (-jnp.arange(n_freq, dtype=F32) / n_freq)
        ang = jnp.concatenate([row[:, None] * inv[None], col[:, None] * inv[None]], axis=-1)
        return jnp.cos(ang), jnp.sin(ang)

    c, s = cos_sin(HEAD_DIM)
    tables = {"c128": jnp.concatenate([c, c], 1), "s128": jnp.concatenate([-s, s], 1)}
    c, s = cos_sin(A_QK_DIM)
    z = jnp.zeros_like(c)
    tables["ca"] = jnp.concatenate([c, c, c, c], 1)
    tables["saa"] = jnp.concatenate([-s, z, -s, z], 1)
    tables["sab"] = jnp.concatenate([z, s, z, s], 1)
    tables["cd"] = jnp.concatenate([c, c, z, z], 1)
    tables["sda"] = jnp.concatenate([-s, z, z, z], 1)
    tables["sdb"] = jnp.concatenate([z, s, z, z], 1)
    return tables


def _odd_weights(w):
    w = w.astype(BF16)
    d0 = C_IN
    gate0 = C_IN + D_IN
    pieces = [w[..., :d0 + D_Q_RANK], w[..., gate0:], w[..., d0 + D_Q_RANK:gate0]]
    used = sum(p.shape[-1] for p in pieces)
    pieces.append(jnp.zeros(w.shape[:-1] + (ODD_PAD - used,), BF16))
    return jnp.concatenate(pieces, axis=-1)


def kernel(x, c, ctx, c_ctx, w_mod, b_mod, norm_g, w_o, final_g, e_w_in, a_lam_q1, a_lam_k1, a_lam_q2, a_lam_k2, a_sub_g, b_sink, o_w_in, c_q_g, c_k_g, d_q_a_g, d_kv_a_g, d_w_q_b, d_w_kv_b):
    xl = x.reshape(N_LAT_ROWS, D_MODEL)
    xc = ctx.reshape(N_CTX_ROWS, D_MODEL)
    cc = jnp.concatenate([c, c_ctx[None], jnp.zeros((8 - BATCH - 1, D_MODEL), F32)], axis=0)
    mod = _modulation(cc, w_mod, b_mod)
    tables = _rope_tables()
    w_out = w_o.astype(BF16)
    w_even = e_w_in.astype(BF16)
    w_odd = _odd_weights(o_w_in)

    for l in range(DEPTH):
        with_ctx = l < DEPTH - 1
        i = l // 2
        shift, scale, gate = (mod[l, :BATCH, k * D_MODEL:(k + 1) * D_MODEL][:, None, :]
                              for k in range(3))
        cshift, cscale, cgate = (mod[l, BATCH:BATCH + 1, k * D_MODEL:(k + 1) * D_MODEL][:, None, :]
                                 for k in range(3))
        g = norm_g[l][None]
        if l % 2 == 0:
            u = _in_proj(xl, scale, shift, g, w_even, i, "in_proj_even")
            uc = _in_proj(xc, cscale, cshift, g, w_even, i, "in_proj_even_ctx")
            lam_init = 0.8 - 0.6 * math.exp(-0.3 * l)
            lam_vecs = jnp.stack([a_lam_q1[i], a_lam_k1[i], a_lam_q2[i], a_lam_k2[i]]).astype(F32)
            sub_g = a_sub_g[i][None]
            y1 = _mixer_a(u, uc, lam_vecs, sub_g, tables, lam_init, True)
            y2 = _mixer_b(u, uc, b_sink[i], tables, True)
            if with_ctx:
                yc1 = _mixer_a(u, uc, lam_vecs, sub_g, tables, lam_init, False)
                yc2 = _mixer_b(u, uc, b_sink[i], tables, False)
        else:
            u = _in_proj(xl, scale, shift, g, w_odd, i, "in_proj_odd")
            uc = _in_proj(xc, cscale, cshift, g, w_odd, i, "in_proj_odd_ctx")
            wq = jnp.pad(d_w_q_b[i].reshape(D_Q_RANK, D_HEADS, D_NOPE + D_ROPE),
                         ((0, 0), (0, 0), (0, 2 * LANES - D_NOPE - D_ROPE)))
            wq = wq.reshape(D_Q_RANK, D_HEADS * 2 * LANES).astype(BF16)
            wkv = d_w_kv_b[i].astype(BF16)
            qd, kvd = _dproj(u, d_q_a_g[i][None], d_kv_a_g[i][None], wq, wkv, "dproj")
            qdc, kvdc = _dproj(uc, d_q_a_g[i][None], d_kv_a_g[i][None], wq, wkv, "dproj_ctx")
            y1 = _mixer_c(u, uc, c_q_g[i][None], c_k_g[i][None], tables, True)
            y2 = _mixer_d(u, uc, qd, kvd, qdc, kvdc, tables, True)
            if with_ctx:
                yc1 = _mixer_c(u, uc, c_q_g[i][None], c_k_g[i][None], tables, False)
                yc2 = _mixer_d(u, uc, qd, kvd, qdc, kvdc, tables, False)
        last = l == DEPTH - 1
        xl = _out_proj(xl, y1, y2, w_out, l, gate, final_g[None] if last else None,
                       "out_proj_final" if last else "out_proj")
        if with_ctx:
            xc = _out_proj(xc, yc1, yc2, w_out, l, cgate, None, "out_proj_ctx")
    return xl.reshape(BATCH, SEQ, D_MODEL)
```
